```python
import math
import jax, jax.numpy as jnp
from jax import lax
import numpy as np

D_MODEL = 4096
BATCH = 4
SEQ = 2048
DEPTH = 2
DEC_BATCH = 8
DEC_SEQ = 1
PAST_LEN = 16384
PAGE_SIZE = 128

N_GROUPS = 4
GROUP_W = D_MODEL // N_GROUPS
D_MIX = N_GROUPS * GROUP_W

RWKV_HS = 64
RWKV_HEADS = GROUP_W // RWKV_HS
RWKV_DECAY_LORA = max(32, int(round(1.8 * GROUP_W ** 0.5 / 32)) * 32)
RWKV_A_LORA = max(32, int(round(1.8 * GROUP_W ** 0.5 / 32)) * 32)
RWKV_G_LORA = max(32, int(round(0.6 * GROUP_W ** 0.8 / 32)) * 32)
RWKV_PROJ = 3 * GROUP_W + RWKV_DECAY_LORA + RWKV_A_LORA + RWKV_G_LORA
RWKV_GN_EPS = 64e-5

LRU_W = GROUP_W
LRU_BLOCKS = 16
LRU_BS = LRU_W // LRU_BLOCKS
CONV_W = 4
LRU_C = 8.0

DIFF_HEADS = 4
DIFF_DH = GROUP_W // (2 * DIFF_HEADS)
DIFF_VD = 2 * DIFF_DH

FOX_HEADS = 8
FOX_DH = GROUP_W // FOX_HEADS

ROPE_THETA = 500000.0
ROPE_DIMS = DIFF_DH // 4

D_FF = 7 * D_MODEL // 2
N_EXPERTS = 8
TOP_K = 2
N_DENSE = (DEPTH + 1) // 2
N_MOE = DEPTH // 2

Q_BLOCK = 128
MOE_BLOCK = 128
NORM_EPS = 1e-5

IN_SPLITS = (
    ('rwkv_r', GROUP_W), ('rwkv_k', GROUP_W), ('rwkv_v', GROUP_W),
    ('rwkv_w', RWKV_DECAY_LORA), ('rwkv_a', RWKV_A_LORA), ('rwkv_g', RWKV_G_LORA),
    ('lru_x', LRU_W), ('lru_gate', LRU_W),
    ('diff_q', 2 * DIFF_HEADS * DIFF_DH), ('diff_k', 2 * DIFF_HEADS * DIFF_DH), ('diff_v', DIFF_HEADS * DIFF_VD),
    ('fox_q', FOX_HEADS * FOX_DH), ('fox_k', FOX_HEADS * FOX_DH), ('fox_v', FOX_HEADS * FOX_DH), ('fox_f', FOX_HEADS),
)
N_IN = sum(w for _, w in IN_SPLITS)

kernel_name = 'hybrid_rwkv7_rglru_diffattn_fox_decode_step'


def rmsnorm(x, g, eps=NORM_EPS):
    xf = x.astype(jnp.float32)
    y = xf * lax.rsqrt(jnp.mean(xf * xf, axis=-1, keepdims=True) + eps)
    return (y * g.astype(jnp.float32)).astype(x.dtype)


def split_cols(z):
    parts, off = {}, 0
    for name, width in IN_SPLITS:
        parts[name] = z[..., off:off + width]
        off += width
    return parts


def apply_partial_rope(x, pos):
    half = ROPE_DIMS // 2
    inv = ROPE_THETA ** (-jnp.arange(half, dtype=jnp.float32) * (2.0 / ROPE_DIMS))
    ang = pos.astype(jnp.float32)[:, None] * inv[None, :]
    cos = jnp.cos(ang)[None, :, None, None, :]
    sin = jnp.sin(ang)[None, :, None, None, :]
    xf = x.astype(jnp.float32)
    x1, x2 = xf[..., :half], xf[..., half:ROPE_DIMS]
    out = jnp.concatenate([x1 * cos - x2 * sin, x2 * cos + x1 * sin, xf[..., ROPE_DIMS:]], axis=-1)
    return out.astype(x.dtype)


def causal_attention(q, k, v, q_pos, k_pos, cum_q=None, cum_k=None):
    B, Sq, H, C, d = q.shape
    blk = Q_BLOCK if Sq % Q_BLOCK == 0 else Sq
    nb = Sq // blk
    scale = d ** -0.5
    kf = k.astype(jnp.float32)
    vf = v.astype(jnp.float32)
    use_decay = cum_q is not None
    ck_t = jnp.swapaxes(cum_k, 1, 2) if use_decay else None

    def one_block(args):
        qb, pb = args[0], args[1]
        s = jnp.einsum('bqhcd,bkhcd->bhcqk', qb.astype(jnp.float32), kf) * scale
        if use_decay:
            cb = jnp.swapaxes(args[2], 1, 2)
            s = s + (cb[:, :, None, :, None] - ck_t[:, :, None, None, :])
        mask = k_pos[None, :] <= pb[:, None]
        s = jnp.where(mask, s, -jnp.inf)
        p = jax.nn.softmax(s, axis=-1)
        return jnp.einsum('bhcqk,bkhv->bqhcv', p, vf)

    qs = jnp.moveaxis(q.reshape(B, nb, blk, H, C, d), 1, 0)
    ps = q_pos.reshape(nb, blk)
    if use_decay:
        xs = (qs, ps, jnp.moveaxis(cum_q.reshape(B, nb, blk, H), 1, 0))
    else:
        xs = (qs, ps)
    out = lax.map(one_block, xs)
    return jnp.moveaxis(out, 0, 1).reshape(B, Sq, H, C, -1)


def rwkv7_time_mix(zr, shift0, wkv0, mu, w0, w2, a0, a2, g2, k_k, k_a, r_k, ln_w, ln_b):
    B, S, _ = zr.shape
    G, H, N = GROUP_W, RWKV_HEADS, RWKV_HS
    prev = jnp.concatenate([shift0[:, None].astype(zr.dtype), zr[:, :-1]], axis=1)
    zs = (zr + (prev - zr) * mu).astype(jnp.float32)
    o1 = 3 * G
    o2 = o1 + RWKV_DECAY_LORA
    o3 = o2 + RWKV_A_LORA
    r, k, v = zs[..., :G], zs[..., G:2 * G], zs[..., 2 * G:3 * G]
    w_raw = -jax.nn.softplus(-(w0 + jnp.tanh(zs[..., o1:o2]) @ w2)) - 0.5
    decay = jnp.exp(-jnp.exp(w_raw))
    a = jax.nn.sigmoid(a0 + zs[..., o2:o3] @ a2)
    g = jax.nn.sigmoid(zs[..., o3:]) @ g2
    kk = (k * k_k).reshape(B, S, H, N)
    kk = kk / jnp.maximum(jnp.linalg.norm(kk, axis=-1, keepdims=True), 1e-12)
    k = (k * (1.0 + (a - 1.0) * k_a)).reshape(B, S, H, N)
    r = r.reshape(B, S, H, N)
    v = v.reshape(B, S, H, N)
    decay = decay.reshape(B, S, H, N)
    a = a.reshape(B, S, H, N)

    def step(state, inp):
        r_t, w_t, k_t, v_t, kk_t, a_t = inp
        sa = jnp.einsum('bhij,bhj->bhi', state, -kk_t)
        state = (state * w_t[:, :, None, :] + sa[..., None] * (kk_t * a_t)[:, :, None, :]
                 + v_t[..., None] * k_t[:, :, None, :])
        return state, jnp.einsum('bhij,bhj->bhi', state, r_t)

    xs = tuple(jnp.swapaxes(t, 0, 1) for t in (r, decay, k, v, kk, a))
    wkv_n, out = lax.scan(step, wkv0.astype(jnp.float32), xs)
    out = jnp.swapaxes(out, 0, 1)
    mean = jnp.mean(out, axis=-1, keepdims=True)
    var = jnp.mean(jnp.square(out - mean), axis=-1, keepdims=True)
    out = (out - mean) * lax.rsqrt(var + RWKV_GN_EPS) * ln_w.reshape(H, N) + ln_b.reshape(H, N)
    out = out + jnp.sum(r * k * r_k, axis=-1, keepdims=True) * v
    y = out.reshape(B, S, G) * g
    return y, zr[:, -1], wkv_n


def rg_lru_block(zx, zgate, conv0, h0, conv_w, conv_b, wa, ba, wx, bx, lam):
    B, S, C = zx.shape
    xpad = jnp.concatenate([conv0.astype(zx.dtype), zx], axis=1)
    xc = lax.conv_general_dilated(xpad, conv_w.astype(zx.dtype)[:, None, :], window_strides=(1,),
                                  padding='VALID', dimension_numbers=('NWC', 'WIO', 'NWC'),
                                  feature_group_count=C) + conv_b
    xc = xc.astype(jnp.float32)
    xb = xc.reshape(B, S, LRU_BLOCKS, LRU_BS)
    r = jax.nn.sigmoid(jnp.einsum('bsnc,ncd->bsnd', xb, wa.astype(jnp.float32)).reshape(B, S, C) + ba)
    i = jax.nn.sigmoid(jnp.einsum('bsnc,ncd->bsnd', xb, wx.astype(jnp.float32)).reshape(B, S, C) + bx)
    log_a = -LRU_C * r * jax.nn.softplus(-lam.astype(jnp.float32))
    a = jnp.exp(log_a)
    b = jnp.sqrt(-jnp.expm1(2.0 * log_a)) * (i * xc)
    b = b.at[:, 0].add(a[:, 0] * h0.astype(jnp.float32))

    def combine(left, right):
        a1, b1 = left
        a2, b2 = right
        return a1 * a2, a2 * b1 + b2

    _, h = lax.associative_scan(combine, (a, b), axis=1)
    y = h * jax.nn.gelu(zgate.astype(jnp.float32))
    return y, xpad[:, -(CONV_W - 1):], h[:, -1]


def differential_attention(zq, zk, zv, pos, k_past, v_past, lq1, lk1, lq2, lk2, subln_g, layer):
    B, S, _ = zq.shape
    q = apply_partial_rope(zq.reshape(B, S, DIFF_HEADS, 2, DIFF_DH), pos)
    k = apply_partial_rope(zk.reshape(B, S, DIFF_HEADS, 2, DIFF_DH), pos)
    v = zv.reshape(B, S, DIFF_HEADS, DIFF_VD)
    k_all, v_all = k, v
    if k_past is not None:
        P = k_past.shape[1]
        k_all = jnp.concatenate([k_past.reshape(B, P, DIFF_HEADS, 2, DIFF_DH).astype(k.dtype), k], axis=1)
        v_all = jnp.concatenate([v_past.astype(v.dtype), v], axis=1)
    k_pos = jnp.arange(k_all.shape[1], dtype=jnp.int32)
    o = causal_attention(q, k_all, v_all, pos, k_pos)
    lam_init = 0.8 - 0.6 * math.exp(-0.3 * layer)
    lam = (jnp.exp(jnp.sum(lq1.astype(jnp.float32) * lk1.astype(jnp.float32)))
           - jnp.exp(jnp.sum(lq2.astype(jnp.float32) * lk2.astype(jnp.float32))) + lam_init)
    o = o[..., 0, :] - lam * o[..., 1, :]
    o = rmsnorm(o, subln_g) * (1.0 - lam_init)
    return o.reshape(B, S, GROUP_W), k.reshape(B, S, DIFF_HEADS, 2 * DIFF_DH), v


def forgetting_attention(zq, zk, zv, zf, b_f, pos, k_past, v_past, logf_past):
    B, S, _ = zq.shape
    q = zq.reshape(B, S, FOX_HEADS, 1, FOX_DH)
    k = zk.reshape(B, S, FOX_HEADS, FOX_DH)
    v = zv.reshape(B, S, FOX_HEADS, FOX_DH)
    logf = jax.nn.log_sigmoid((zf + b_f).astype(jnp.float32))
    k_all, v_all, lf_all = k, v, logf
    if k_past is not None:
        k_all = jnp.concatenate([k_past.astype(k.dtype), k], axis=1)
        v_all = jnp.concatenate([v_past.astype(v.dtype), v], axis=1)
        lf_all = jnp.concatenate([logf_past.astype(jnp.float32), logf], axis=1)
    cum = jnp.cumsum(lf_all, axis=1)
    k_pos = jnp.arange(k_all.shape[1], dtype=jnp.int32)
    o = causal_attention(q, k_all[:, :, :, None], v_all, pos, k_pos, cum[:, -S:], cum)
    return o[:, :, :, 0].reshape(B, S, GROUP_W), k, v, logf


def swiglu(x, w_gate, w_up, w_down):
    return (jax.nn.silu(x @ w_gate) * (x @ w_up)) @ w_down


def moe_swiglu(x, w_router, b_router, w_gate, w_up, w_down, layer_idx):
    B, S, D = x.shape
    xt = x.reshape(-1, D)
    n = xt.shape[0]
    logits = xt.astype(jnp.float32) @ w_router.astype(jnp.float32) + b_router.astype(jnp.float32)
    top_val, top_idx = lax.top_k(logits, TOP_K)
    gates = jax.nn.softmax(top_val, axis=-1)
    m = n * TOP_K
    blk = min(MOE_BLOCK, m)
    nblk = -(-m // blk) + N_EXPERTS
    e_flat = top_idx.reshape(-1)
    tok = jnp.arange(m, dtype=jnp.int32) // TOP_K
    order = jnp.argsort(e_flat)
    e_sorted = e_flat[order]
    counts = jnp.bincount(e_flat, length=N_EXPERTS)
    pcounts = ((counts + blk - 1) // blk) * blk
    pend = jnp.cumsum(pcounts)
    pstart = pend - pcounts
    start = jnp.cumsum(counts) - counts
    dest = pstart[e_sorted] + (jnp.arange(m) - start[e_sorted])
    slot_tok = jnp.full((nblk * blk,), n, jnp.int32).at[dest].set(tok[order])
    slot_gate = jnp.zeros((nblk * blk,), jnp.float32).at[dest].set(gates.reshape(-1)[order])
    block_e = jnp.minimum(jnp.searchsorted(pend, jnp.arange(nblk) * blk, side='right'), N_EXPERTS - 1)
    x_pad = jnp.concatenate([xt, jnp.zeros((1, D), xt.dtype)], axis=0)
    xb = x_pad[slot_tok].reshape(nblk, blk, D)

    def run(args):
        xs, e = args
        return swiglu(xs, w_gate[layer_idx, e], w_up[layer_idx, e], w_down[layer_idx, e])

    yb = lax.map(run, (xb, block_e)).reshape(nblk * blk, D)
    y = jnp.zeros((n + 1, D), jnp.float32).at[slot_tok].add(yb.astype(jnp.float32) * slot_gate[:, None])[:n]
    return y.astype(x.dtype).reshape(B, S, D)


def gather_pages(pool, layer, page_table):
    g = pool[layer, page_table]
    return g.reshape((g.shape[0], g.shape[1] * g.shape[2]) + g.shape[3:])


def setup_inputs(seed: int = 0) -> dict:
    key = jax.random.key(seed)
    keys = iter(jax.random.split(key, 64))
    f32 = jnp.float32

    def normal(shape, scale=1.0):
        return jax.random.normal(next(keys), shape, f32) * scale

    def uniform(shape, lo, hi):
        return jax.random.uniform(next(keys), shape, f32, lo, hi)

    n_pages = PAST_LEN // PAGE_SIZE
    n_used = DEC_BATCH * n_pages
    n_phys = n_used + max(1, n_used // 4)
    page_table = jax.random.permutation(next(keys), n_phys)[:n_used].reshape(DEC_BATCH, n_pages).astype(jnp.int32)
    u = uniform((DEPTH, LRU_W), 0.9, 0.999)
    a_base = u ** (1.0 / LRU_C)
    lru_lambda = jnp.log(a_base) - jnp.log1p(-a_base)
    return {
        'x_prompt': normal((BATCH, SEQ, D_MODEL)),
        'x_sample': normal((DEC_BATCH, DEC_SEQ, D_MODEL)),
        'cache_diff_k': normal((DEPTH, n_phys, PAGE_SIZE, DIFF_HEADS, 2 * DIFF_DH)),
        'cache_diff_v': normal((DEPTH, n_phys, PAGE_SIZE, DIFF_HEADS, DIFF_VD)),
        'cache_fox_k': normal((DEPTH, n_phys, PAGE_SIZE, FOX_HEADS, FOX_DH)),
        'cache_fox_v': normal((DEPTH, n_phys, PAGE_SIZE, FOX_HEADS, FOX_DH)),
        'cache_fox_logf': jax.nn.log_sigmoid(normal((DEPTH, n_phys, PAGE_SIZE, FOX_HEADS)) + 3.0),
        'state_rwkv_shift': normal((DEPTH, DEC_BATCH, RWKV_PROJ)),
        'state_rwkv_wkv': normal((DEPTH, DEC_BATCH, RWKV_HEADS, RWKV_HS, RWKV_HS), 0.3),
        'state_lru_conv': normal((DEPTH, DEC_BATCH, CONV_W - 1, LRU_W)),
        'state_lru_h': normal((DEPTH, DEC_BATCH, LRU_W), 0.5),
        'page_table': page_table,
        'norm_mix': 1.0 + normal((DEPTH, D_MODEL), 0.05),
        'norm_ffn': 1.0 + normal((DEPTH, D_MODEL), 0.05),
        'norm_final': 1.0 + normal((D_MODEL,), 0.05),
        'w_in': normal((DEPTH, D_MODEL, N_IN), D_MODEL ** -0.5),
        'w_out': normal((DEPTH, D_MIX, D_MODEL), D_MIX ** -0.5),
        'rwkv_mu': uniform((DEPTH, RWKV_PROJ), 0.0, 1.0),
        'rwkv_w0': uniform((DEPTH, GROUP_W), -6.0, -1.0),
        'rwkv_w2': normal((DEPTH, RWKV_DECAY_LORA, GROUP_W), 0.1),
        'rwkv_a0': normal((DEPTH, GROUP_W), 0.1),
        'rwkv_a2': normal((DEPTH, RWKV_A_LORA, GROUP_W), RWKV_A_LORA ** -0.5),
        'rwkv_g2': normal((DEPTH, RWKV_G_LORA, GROUP_W), RWKV_G_LORA ** -0.5),
        'rwkv_k_k': 0.85 + normal((DEPTH, GROUP_W), 0.05),
        'rwkv_k_a': 1.0 + normal((DEPTH, GROUP_W), 0.05),
        'rwkv_r_k': normal((DEPTH, RWKV_HEADS, RWKV_HS), 0.1),
        'rwkv_ln_w': 1.0 + normal((DEPTH, GROUP_W), 0.05),
        'rwkv_ln_b': normal((DEPTH, GROUP_W), 0.01),
        'lru_conv_w': normal((DEPTH, CONV_W, LRU_W), CONV_W ** -0.5),
        'lru_conv_b': normal((DEPTH, LRU_W), 0.01),
        'lru_wa': normal((DEPTH, LRU_BLOCKS, LRU_BS, LRU_BS), LRU_BS ** -0.5),
        'lru_ba': normal((DEPTH, LRU_W), 0.01),
        'lru_wx': normal((DEPTH, LRU_BLOCKS, LRU_BS, LRU_BS), LRU_BS ** -0.5),
        'lru_bx': normal((DEPTH, LRU_W), 0.01),
        'lru_lambda': lru_lambda,
        'diff_lq1': normal((DEPTH, DIFF_DH), 0.1),
        'diff_lk1': normal((DEPTH, DIFF_DH), 0.1),
        'diff_lq2': normal((DEPTH, DIFF_DH), 0.1),
        'diff_lk2': normal((DEPTH, DIFF_DH), 0.1),
        'diff_subln': 1.0 + normal((DEPTH, DIFF_VD), 0.05),
        'fox_bf': uniform((DEPTH, FOX_HEADS), 2.0, 4.0),
        'ffn_w_gate': normal((N_DENSE, D_MODEL, D_FF), D_MODEL ** -0.5),
        'ffn_w_up': normal((N_DENSE, D_MODEL, D_FF), D_MODEL ** -0.5),
        'ffn_w_down': normal((N_DENSE, D_FF, D_MODEL), D_FF ** -0.5),
        'moe_router': normal((N_MOE, D_MODEL, N_EXPERTS), D_MODEL ** -0.5),
        'moe_router_b': normal((N_MOE, N_EXPERTS), 0.01),
        'moe_w_gate': normal((N_MOE, N_EXPERTS, D_MODEL, D_FF), D_MODEL ** -0.5),
        'moe_w_up': normal((N_MOE, N_EXPERTS, D_MODEL, D_FF), D_MODEL ** -0.5),
        'moe_w_down': normal((N_MOE, N_EXPERTS, D_FF, D_MODEL), D_FF ** -0.5),
    }


def reference(x_prompt, x_sample, cache_diff_k, cache_diff_v, cache_fox_k, cache_fox_v, cache_fox_logf,
              state_rwkv_shift, state_rwkv_wkv, state_lru_conv, state_lru_h, page_table,
              norm_mix, norm_ffn, norm_final, w_in, w_out,
              rwkv_mu, rwkv_w0, rwkv_w2, rwkv_a0, rwkv_a2, rwkv_g2, rwkv_k_k, rwkv_k_a, rwkv_r_k,
              rwkv_ln_w, rwkv_ln_b,
              lru_conv_w, lru_conv_b, lru_wa, lru_ba, lru_wx, lru_bx, lru_lambda,
              diff_lq1, diff_lk1, diff_lq2, diff_lk2, diff_subln, fox_bf,
              ffn_w_gate, ffn_w_up, ffn_w_down,
              moe_router, moe_router_b, moe_w_gate, moe_w_up, moe_w_down):

    def mixer(h, l, pos, shift0, wkv0, conv0, lru_h0, past):
        z = h @ w_in[l]
        c = split_cols(z)
        ya, shift_n, wkv_n = rwkv7_time_mix(
            z[..., :RWKV_PROJ], shift0, wkv0, rwkv_mu[l], rwkv_w0[l], rwkv_w2[l], rwkv_a0[l], rwkv_a2[l],
            rwkv_g2[l], rwkv_k_k[l], rwkv_k_a[l], rwkv_r_k[l], rwkv_ln_w[l], rwkv_ln_b[l])
        yb, conv_n, lru_hn = rg_lru_block(
            c['lru_x'], c['lru_gate'], conv0, lru_h0, lru_conv_w[l], lru_conv_b[l],
            lru_wa[l], lru_ba[l], lru_wx[l], lru_bx[l], lru_lambda[l])
        dk_p, dv_p, fk_p, fv_p, fl_p = past
        yc, dk, dv = differential_attention(
            c['diff_q'], c['diff_k'], c['diff_v'], pos, dk_p, dv_p,
            diff_lq1[l], diff_lk1[l], diff_lq2[l], diff_lk2[l], diff_subln[l], l)
        yd, fk, fv, fl = forgetting_attention(
            c['fox_q'], c['fox_k'], c['fox_v'], c['fox_f'], fox_bf[l], pos, fk_p, fv_p, fl_p)
        y = jnp.concatenate([ya.astype(h.dtype), yb.astype(h.dtype), yc.astype(h.dtype), yd.astype(h.dtype)],
                            axis=-1) @ w_out[l]
        return y, (dk, dv, fk, fv, fl, shift_n, wkv_n, conv_n, lru_hn)

    def channel_mix(h, l):
        i = l // 2
        if l % 2 == 0:
            return swiglu(h, ffn_w_gate[i], ffn_w_up[i], ffn_w_down[i])
        return moe_swiglu(h, moe_router[i], moe_router_b[i], moe_w_gate, moe_w_up, moe_w_down, i)

    def trunk(x, pos, layer_state):
        new = []
        for l in range(DEPTH):
            shift0, wkv0, conv0, lru_h0, past = layer_state(l)
            m, st = mixer(rmsnorm(x, norm_mix[l]), l, pos, shift0, wkv0, conv0, lru_h0, past)
            x = x + m
            x = x + channel_mix(rmsnorm(x, norm_ffn[l]), l)
            new.append(st)
        return rmsnorm(x, norm_final), [jnp.stack([s[i] for s in new]) for i in range(len(new[0]))]

    bp, sp, _ = x_prompt.shape
    dtp = x_prompt.dtype

    def prompt_state(l):
        return (jnp.zeros((bp, RWKV_PROJ), dtp),
                jnp.zeros((bp, RWKV_HEADS, RWKV_HS, RWKV_HS), jnp.float32),
                jnp.zeros((bp, CONV_W - 1, LRU_W), dtp),
                jnp.zeros((bp, LRU_W), jnp.float32),
                (None, None, None, None, None))

    def sample_state(l):
        past = (gather_pages(cache_diff_k, l, page_table), gather_pages(cache_diff_v, l, page_table),
                gather_pages(cache_fox_k, l, page_table), gather_pages(cache_fox_v, l, page_table),
                gather_pages(cache_fox_logf, l, page_table))
        return (state_rwkv_shift[l], state_rwkv_wkv[l], state_lru_conv[l], state_lru_h[l], past)

    past_len = page_table.shape[1] * cache_diff_k.shape[2]
    pos_p = jnp.arange(sp, dtype=jnp.int32)
    pos_s = past_len + jnp.arange(x_sample.shape[1], dtype=jnp.int32)

    y_prompt, (p_diff_k, p_diff_v, p_fox_k, p_fox_v, p_fox_logf,
               p_rwkv_shift, p_rwkv_wkv, p_lru_conv, p_lru_h) = trunk(x_prompt, pos_p, prompt_state)
    y_sample, (s_diff_k, s_diff_v, s_fox_k, s_fox_v, s_fox_logf,
               s_rwkv_shift, s_rwkv_wkv, s_lru_conv, s_lru_h) = trunk(x_sample, pos_s, sample_state)

    return (y_prompt, y_sample,
            p_diff_k, p_diff_v, p_fox_k, p_fox_v, p_fox_logf, p_rwkv_shift, p_rwkv_wkv, p_lru_conv, p_lru_h,
            s_diff_k, s_diff_v, s_fox_k, s_fox_v, s_fox_logf, s_rwkv_shift, s_rwkv_wkv, s_lru_conv, s_lru_h)
```

```python
import functools
import math

import jax
import jax.numpy as jnp
from jax import lax
from jax.experimental import pallas as pl
from jax.experimental.pallas import tpu as pltpu

F32 = jnp.float32
BF16 = jnp.bfloat16

RWKV_HS = 64
RWKV_GN_EPS = 64e-5
LRU_C = 8.0
ROPE_THETA = 500000.0
NORM_EPS = 1e-5
TOP_K = 2
Q_BLOCK = 128

V7X_VMEM_LIMIT_BYTES = 52 * 1024 * 1024
LANE = 128


def _pick(n, candidates):
    for c in candidates:
        if n % c == 0:
            return c
    return n


def _cparams(sem):
    return pltpu.CompilerParams(dimension_semantics=sem, vmem_limit_bytes=V7X_VMEM_LIMIT_BYTES)


def _mm_kernel(x_ref, w_ref, o_ref):
    o_ref[...] = jnp.dot(x_ref[...], w_ref[...], preferred_element_type=F32).astype(o_ref.dtype)


def _mm_res_kernel(x_ref, w_ref, r_ref, o_ref):
    o_ref[...] = r_ref[...] + jnp.dot(x_ref[...], w_ref[...], preferred_element_type=F32)


def _mm(x, w, resid=None, out_dtype=F32):
    M, K = x.shape
    N = w.shape[1]
    tm = _pick(M, (640, 512, 256, 128))
    tn = _pick(N, (512, 256, 128))
    in_specs = [pl.BlockSpec((tm, K), lambda i, j: (i, 0)),
                pl.BlockSpec((K, tn), lambda i, j: (0, j))]
    args = [x, w]
    body = _mm_kernel
    if resid is not None:
        in_specs.append(pl.BlockSpec((tm, tn), lambda i, j: (i, j)))
        args.append(resid)
        body = _mm_res_kernel
    return pl.pallas_call(
        body,
        out_shape=jax.ShapeDtypeStruct((M, N), out_dtype),
        grid=(M // tm, N // tn),
        in_specs=in_specs,
        out_specs=pl.BlockSpec((tm, tn), lambda i, j: (i, j)),
        compiler_params=_cparams(("parallel", "arbitrary")),
        name="mm_res" if resid is not None else "mm",
    )(*args)


def _gateup_kernel(x_ref, wg_ref, wu_ref, o_ref):
    x = x_ref[...]
    g = jnp.dot(x, wg_ref[...], preferred_element_type=F32)
    u = jnp.dot(x, wu_ref[...], preferred_element_type=F32)
    o_ref[...] = (g * jax.nn.sigmoid(g) * u).astype(o_ref.dtype)


def _gateup(x, wg, wu):
    M, K = x.shape
    F = wg.shape[1]
    tm = _pick(M, (640, 512, 256, 128))
    tn = _pick(F, (512, 256, 128))
    return pl.pallas_call(
        _gateup_kernel,
        out_shape=jax.ShapeDtypeStruct((M, F), BF16),
        grid=(M // tm, F // tn),
        in_specs=[pl.BlockSpec((tm, K), lambda i, j: (i, 0)),
                  pl.BlockSpec((K, tn), lambda i, j: (0, j)),
                  pl.BlockSpec((K, tn), lambda i, j: (0, j))],
        out_specs=pl.BlockSpec((tm, tn), lambda i, j: (i, j)),
        compiler_params=_cparams(("parallel", "arbitrary")),
        name="ffn_gateup",
    )(x, wg, wu)


def _down_kernel(x_ref, w_ref, r_ref, o_ref, acc_ref):
    k = pl.program_id(2)

    @pl.when(k == 0)
    def _():
        acc_ref[...] = r_ref[...]

    acc_ref[...] += jnp.dot(x_ref[...], w_ref[...], preferred_element_type=F32)

    @pl.when(k == pl.num_programs(2) - 1)
    def _():
        o_ref[...] = acc_ref[...]


def _down(h, w, resid):
    M, Fd = h.shape
    N = w.shape[1]
    tm = _pick(M, (640, 512, 256, 128))
    tn = _pick(N, (1024, 512, 256, 128))
    tk = _pick(Fd, (2048, 1024, 512, 256, 128))
    return pl.pallas_call(
        _down_kernel,
        out_shape=jax.ShapeDtypeStruct((M, N), F32),
        grid=(M // tm, N // tn, Fd // tk),
        in_specs=[pl.BlockSpec((tm, tk), lambda i, j, k: (i, k)),
                  pl.BlockSpec((tk, tn), lambda i, j, k: (k, j)),
                  pl.BlockSpec((tm, tn), lambda i, j, k: (i, j))],
        out_specs=pl.BlockSpec((tm, tn), lambda i, j, k: (i, j)),
        scratch_shapes=[pltpu.VMEM((tm, tn), F32)],
        compiler_params=_cparams(("parallel", "parallel", "arbitrary")),
        name="ffn_down",
    )(h, w, resid)


def _moe_gateup_kernel(be_ref, nu_ref, x_ref, wg_ref, wu_ref, o_ref):
    i = pl.program_id(1)

    @pl.when(i < nu_ref[0])
    def _():
        x = x_ref[...]
        g = jnp.dot(x, wg_ref[0], preferred_element_type=F32)
        u = jnp.dot(x, wu_ref[0], preferred_element_type=F32)
        o_ref[...] = (g * jax.nn.sigmoid(g) * u).astype(o_ref.dtype)

    @pl.when(i >= nu_ref[0])
    def _():
        o_ref[...] = jnp.zeros_like(o_ref)


def _moe_gateup(block_e, n_used, xs, wg, wu, blk):
    Mp, K = xs.shape
    F = wg.shape[2]
    nblk = Mp // blk
    tn = _pick(F, (512, 256, 128))
    grid_spec = pltpu.PrefetchScalarGridSpec(
        num_scalar_prefetch=2,
        grid=(F // tn, nblk),
        in_specs=[pl.BlockSpec((blk, K), lambda j, i, be, nu: (i, 0)),
                  pl.BlockSpec((1, K, tn), lambda j, i, be, nu: (be[i], 0, j)),
                  pl.BlockSpec((1, K, tn), lambda j, i, be, nu: (be[i], 0, j))],
        out_specs=pl.BlockSpec((blk, tn), lambda j, i, be, nu: (i, j)),
    )
    return pl.pallas_call(
        _moe_gateup_kernel,
        out_shape=jax.ShapeDtypeStruct((Mp, F), BF16),
        grid_spec=grid_spec,
        compiler_params=_cparams(("arbitrary", "arbitrary")),
        name="moe_gateup",
    )(block_e, n_used, xs, wg, wu)


def _moe_down_kernel(be_ref, nu_ref, h_ref, w_ref, g_ref, o_ref, acc_ref):
    i = pl.program_id(1)
    k = pl.program_id(2)
    last = pl.num_programs(2) - 1
    used = i < nu_ref[0]

    @pl.when(jnp.logical_and(used, k == 0))
    def _():
        acc_ref[...] = jnp.zeros_like(acc_ref)

    @pl.when(used)
    def _():
        acc_ref[...] += jnp.dot(h_ref[...], w_ref[0], preferred_element_type=F32)

    @pl.when(jnp.logical_and(used, k == last))
    def _():
        o_ref[...] = acc_ref[...] * g_ref[...]

    @pl.when(jnp.logical_and(jnp.logical_not(used), k == last))
    def _():
        o_ref[...] = jnp.zeros_like(o_ref)


def _moe_down(block_e, n_used, hs, wd, slot_gate, blk):
    Mp, Fd = hs.shape
    N = wd.shape[2]
    nblk = Mp // blk
    tn = _pick(N, (2048, 1024, 512, 256, 128))
    tk = _pick(Fd, (2048, 1024, 512, 256, 128))
    grid_spec = pltpu.PrefetchScalarGridSpec(
        num_scalar_prefetch=2,
        grid=(N // tn, nblk, Fd // tk),
        in_specs=[pl.BlockSpec((blk, tk), lambda j, i, k, be, nu: (i, k)),
                  pl.BlockSpec((1, tk, tn), lambda j, i, k, be, nu: (be[i], k, j)),
                  pl.BlockSpec((blk, 1), lambda j, i, k, be, nu: (i, 0))],
        out_specs=pl.BlockSpec((blk, tn), lambda j, i, k, be, nu: (i, j)),
        scratch_shapes=[pltpu.VMEM((blk, tn), F32)],
    )
    return pl.pallas_call(
        _moe_down_kernel,
        out_shape=jax.ShapeDtypeStruct((Mp, N), F32),
        grid_spec=grid_spec,
        compiler_params=_cparams(("arbitrary", "arbitrary", "arbitrary")),
        name="moe_down",
    )(block_e, n_used, hs, wd, slot_gate)


def _rmsnorm(x, g, eps=NORM_EPS):
    xf = x.astype(F32)
    y = xf * lax.rsqrt(jnp.mean(xf * xf, axis=-1, keepdims=True) + eps)
    return y * g.astype(F32)


def _apply_partial_rope(x, pos, rope_dims):
    half = rope_dims // 2
    inv = ROPE_THETA ** (-jnp.arange(half, dtype=F32) * (2.0 / rope_dims))
    ang = pos.astype(F32)[:, None] * inv[None, :]
    cos = jnp.cos(ang)[None, :, None, None, :]
    sin = jnp.sin(ang)[None, :, None, None, :]
    x1, x2 = x[..., :half], x[..., half:rope_dims]
    return jnp.concatenate([x1 * cos - x2 * sin, x2 * cos + x1 * sin, x[..., rope_dims:]], axis=-1)


def _causal_attention(q, k, v, q_pos, k_pos, cum_q=None, cum_k=None):
    B, Sq, H, C, d = q.shape
    blk = Q_BLOCK if Sq % Q_BLOCK == 0 else Sq
    nb = Sq // blk
    scale = d ** -0.5
    use_decay = cum_q is not None
    ck_t = jnp.swapaxes(cum_k, 1, 2) if use_decay else None

    def one_block(args):
        qb, pb = args[0], args[1]
        s = jnp.einsum('bqhcd,bkhcd->bhcqk', qb, k) * scale
        if use_decay:
            cb = jnp.swapaxes(args[2], 1, 2)
            s = s + (cb[:, :, None, :, None] - ck_t[:, :, None, None, :])
        mask = k_pos[None, :] <= pb[:, None]
        s = jnp.where(mask, s, -jnp.inf)
        p = jax.nn.softmax(s, axis=-1)
        return jnp.einsum('bhcqk,bkhv->bqhcv', p, v)

    qs = jnp.moveaxis(q.reshape(B, nb, blk, H, C, d), 1, 0)
    ps = q_pos.reshape(nb, blk)
    if use_decay:
        xs = (qs, ps, jnp.moveaxis(cum_q.reshape(B, nb, blk, H), 1, 0))
    else:
        xs = (qs, ps)
    out = lax.map(one_block, xs)
    return jnp.moveaxis(out, 0, 1).reshape(B, Sq, H, C, -1)


def _rwkv7_time_mix(zr, shift0, wkv0, mu, w0, w2, a0, a2, g2, k_k, k_a, r_k, ln_w, ln_b):
    B, S, _ = zr.shape
    G = w0.shape[0]
    N = RWKV_HS
    H = G // N
    dl, al = w2.shape[0], a2.shape[0]
    prev = jnp.concatenate([shift0[:, None], zr[:, :-1]], axis=1)
    zs = zr + (prev - zr) * mu
    o1 = 3 * G
    o2 = o1 + dl
    o3 = o2 + al
    r, k, v = zs[..., :G], zs[..., G:2 * G], zs[..., 2 * G:3 * G]
    w_raw = -jax.nn.softplus(-(w0 + jnp.tanh(zs[..., o1:o2]) @ w2)) - 0.5
    decay = jnp.exp(-jnp.exp(w_raw))
    a = jax.nn.sigmoid(a0 + zs[..., o2:o3] @ a2)
    g = jax.nn.sigmoid(zs[..., o3:]) @ g2
    kk = (k * k_k).reshape(B, S, H, N)
    kk = kk / jnp.maximum(jnp.linalg.norm(kk, axis=-1, keepdims=True), 1e-12)
    k = (k * (1.0 + (a - 1.0) * k_a)).reshape(B, S, H, N)
    r = r.reshape(B, S, H, N)
    v = v.reshape(B, S, H, N)
    decay = decay.reshape(B, S, H, N)
    a = a.reshape(B, S, H, N)

    def step(state, inp):
        r_t, w_t, k_t, v_t, kk_t, a_t = inp
        sa = jnp.einsum('bhij,bhj->bhi', state, -kk_t)
        state = (state * w_t[:, :, None, :] + sa[..., None] * (kk_t * a_t)[:, :, None, :]
                 + v_t[..., None] * k_t[:, :, None, :])
        return state, jnp.einsum('bhij,bhj->bhi', state, r_t)

    xs = tuple(jnp.swapaxes(t, 0, 1) for t in (r, decay, k, v, kk, a))
    wkv_n, out = lax.scan(step, wkv0.astype(F32), xs)
    out = jnp.swapaxes(out, 0, 1)
    mean = jnp.mean(out, axis=-1, keepdims=True)
    var = jnp.mean(jnp.square(out - mean), axis=-1, keepdims=True)
    out = (out - mean) * lax.rsqrt(var + RWKV_GN_EPS) * ln_w.reshape(H, N) + ln_b.reshape(H, N)
    out = out + jnp.sum(r * k * r_k, axis=-1, keepdims=True) * v
    y = out.reshape(B, S, G) * g
    return y, zr[:, -1], wkv_n


def _rg_lru_block(zx, zgate, conv0, h0, conv_w, conv_b, wa, ba, wx, bx, lam):
    B, S, C = zx.shape
    cw = conv_w.shape[0]
    nblocks, bs = wa.shape[0], wa.shape[1]
    xpad = jnp.concatenate([conv0, zx], axis=1)
    xc = lax.conv_general_dilated(xpad, conv_w[:, None, :], window_strides=(1,),
                                  padding='VALID', dimension_numbers=('NWC', 'WIO', 'NWC'),
                                  feature_group_count=C) + conv_b
    xb = xc.reshape(B, S, nblocks, bs)
    r = jax.nn.sigmoid(jnp.einsum('bsnc,ncd->bsnd', xb, wa).reshape(B, S, C) + ba)
    i = jax.nn.sigmoid(jnp.einsum('bsnc,ncd->bsnd', xb, wx).reshape(B, S, C) + bx)
    log_a = -LRU_C * r * jax.nn.softplus(-lam)
    a = jnp.exp(log_a)
    b = jnp.sqrt(-jnp.expm1(2.0 * log_a)) * (i * xc)
    b = b.at[:, 0].add(a[:, 0] * h0)

    def combine(left, right):
        a1, b1 = left
        a2, b2 = right
        return a1 * a2, a2 * b1 + b2

    _, h = lax.associative_scan(combine, (a, b), axis=1)
    y = h * jax.nn.gelu(zgate)
    return y, xpad[:, -(cw - 1):], h[:, -1]


def _differential_attention(zq, zk, zv, pos, k_past, v_past, lq1, lk1, lq2, lk2, subln_g, layer, heads):
    B, S, G = zq.shape
    dh = G // (2 * heads)
    vd = 2 * dh
    rope_dims = dh // 4
    q = _apply_partial_rope(zq.reshape(B, S, heads, 2, dh), pos, rope_dims)
    k = _apply_partial_rope(zk.reshape(B, S, heads, 2, dh), pos, rope_dims)
    v = zv.reshape(B, S, heads, vd)
    k_all, v_all = k, v
    if k_past is not None:
        P = k_past.shape[1]
        k_all = jnp.concatenate([k_past.reshape(B, P, heads, 2, dh), k], axis=1)
        v_all = jnp.concatenate([v_past, v], axis=1)
    k_pos = jnp.arange(k_all.shape[1], dtype=jnp.int32)
    o = _causal_attention(q, k_all, v_all, pos, k_pos)
    lam_init = 0.8 - 0.6 * math.exp(-0.3 * layer)
    lam = jnp.exp(jnp.sum(lq1 * lk1)) - jnp.exp(jnp.sum(lq2 * lk2)) + lam_init
    o = o[..., 0, :] - lam * o[..., 1, :]
    o = _rmsnorm(o, subln_g) * (1.0 - lam_init)
    return o.reshape(B, S, G), k.reshape(B, S, heads, 2 * dh), v


def _forgetting_attention(zq, zk, zv, zf, b_f, pos, k_past, v_past, logf_past, heads):
    B, S, G = zq.shape
    dh = G // heads
    q = zq.reshape(B, S, heads, 1, dh)
    k = zk.reshape(B, S, heads, dh)
    v = zv.reshape(B, S, heads, dh)
    logf = jax.nn.log_sigmoid(zf + b_f)
    k_all, v_all, lf_all = k, v, logf
    if k_past is not None:
        k_all = jnp.concatenate([k_past, k], axis=1)
        v_all = jnp.concatenate([v_past, v], axis=1)
        lf_all = jnp.concatenate([logf_past, logf], axis=1)
    cum = jnp.cumsum(lf_all, axis=1)
    k_pos = jnp.arange(k_all.shape[1], dtype=jnp.int32)
    o = _causal_attention(q, k_all[:, :, :, None], v_all, pos, k_pos, cum[:, -S:], cum)
    return o[:, :, :, 0].reshape(B, S, G), k, v, logf


def _gather_pages(pool, layer, page_table):
    g = pool[layer, page_table]
    return g.reshape((g.shape[0], g.shape[1] * g.shape[2]) + g.shape[3:])


def _moe_route(xn, n, w_router, b_router, n_experts, blk):
    logits = jnp.dot(xn[:n], w_router, precision=lax.Precision.HIGHEST) + b_router
    top_val, top_idx = lax.top_k(logits, TOP_K)
    gates = jax.nn.softmax(top_val, axis=-1)
    m = n * TOP_K
    nblk = -(-m // blk) + n_experts
    e_flat = top_idx.reshape(-1)
    tok = jnp.arange(m, dtype=jnp.int32) // TOP_K
    order = jnp.argsort(e_flat)
    e_sorted = e_flat[order]
    counts = jnp.bincount(e_flat, length=n_experts)
    pcounts = ((counts + blk - 1) // blk) * blk
    pend = jnp.cumsum(pcounts)
    pstart = pend - pcounts
    start = jnp.cumsum(counts) - counts
    dest = (pstart[e_sorted] + (jnp.arange(m) - start[e_sorted])).astype(jnp.int32)
    slot_tok = jnp.full((nblk * blk,), n, jnp.int32).at[dest].set(tok[order])
    slot_gate = jnp.zeros((nblk * blk,), F32).at[dest].set(gates.reshape(-1)[order])
    block_e = jnp.minimum(jnp.searchsorted(pend, jnp.arange(nblk) * blk, side='right'),
                          n_experts - 1).astype(jnp.int32)
    pos = jnp.zeros((m,), jnp.int32).at[order].set(dest).reshape(n, TOP_K)
    n_used = (pend[-1] // blk).astype(jnp.int32).reshape(1)
    return slot_tok, slot_gate, block_e, n_used, pos


def kernel(x_prompt, x_sample, cache_diff_k, cache_diff_v, cache_fox_k, cache_fox_v, cache_fox_logf, state_rwkv_shift, state_rwkv_wkv, state_lru_conv, state_lru_h, page_table, norm_mix, norm_ffn, norm_final, w_in, w_out, rwkv_mu, rwkv_w0, rwkv_w2, rwkv_a0, rwkv_a2, rwkv_g2, rwkv_k_k, rwkv_k_a, rwkv_r_k, rwkv_ln_w, rwkv_ln_b, lru_conv_w, lru_conv_b, lru_wa, lru_ba, lru_wx, lru_bx, lru_lambda, diff_lq1, diff_lk1, diff_lq2, diff_lk2, diff_subln, fox_bf, ffn_w_gate, ffn_w_up, ffn_w_down, moe_router, moe_router_b, moe_w_gate, moe_w_up, moe_w_down):
    bp, sp, D = x_prompt.shape
    bs, ss, _ = x_sample.shape
    depth = w_in.shape[0]
    n_in = w_in.shape[2]
    G = D // 4
    rproj = rwkv_mu.shape[1]
    diff_heads = cache_diff_k.shape[3]
    fox_heads = cache_fox_k.shape[3]
    n_experts = moe_w_gate.shape[1]
    cw = lru_conv_w.shape[1]
    past_len = page_table.shape[1] * cache_diff_k.shape[2]

    n_p = bp * sp
    n_s = bs * ss
    n_tok = n_p + n_s
    m_all = -(-n_tok // LANE) * LANE
    n_in_pad = -(-n_in // 512) * 512

    x_all = jnp.concatenate([x_prompt.reshape(n_p, D), x_sample.reshape(n_s, D),
                             jnp.zeros((m_all - n_tok, D), F32)], axis=0)

    pos_p = jnp.arange(sp, dtype=jnp.int32)
    pos_s = past_len + jnp.arange(ss, dtype=jnp.int32)

    offs = {}
    off = 0
    for name, width in (('rwkv', rproj), ('lru_x', G), ('lru_gate', G), ('diff_q', G), ('diff_k', G),
                        ('diff_v', G), ('fox_q', G), ('fox_k', G), ('fox_v', G), ('fox_f', fox_heads)):
        offs[name] = (off, off + width)
        off += width

    def col(z, name):
        a, b = offs[name]
        return z[..., a:b]

    def mixers(z, l, pos, shift0, wkv0, conv0, lru_h0, past):
        ya, shift_n, wkv_n = _rwkv7_time_mix(
            col(z, 'rwkv'), shift0, wkv0, rwkv_mu[l], rwkv_w0[l], rwkv_w2[l], rwkv_a0[l], rwkv_a2[l],
            rwkv_g2[l], rwkv_k_k[l], rwkv_k_a[l], rwkv_r_k[l], rwkv_ln_w[l], rwkv_ln_b[l])
        yb, conv_n, lru_hn = _rg_lru_block(
            col(z, 'lru_x'), col(z, 'lru_gate'), conv0, lru_h0, lru_conv_w[l], lru_conv_b[l],
            lru_wa[l], lru_ba[l], lru_wx[l], lru_bx[l], lru_lambda[l])
        dk_p, dv_p, fk_p, fv_p, fl_p = past
        yc, dk, dv = _differential_attention(
            col(z, 'diff_q'), col(z, 'diff_k'), col(z, 'diff_v'), pos, dk_p, dv_p,
            diff_lq1[l], diff_lk1[l], diff_lq2[l], diff_lk2[l], diff_subln[l], l, diff_heads)
        yd, fk, fv, fl = _forgetting_attention(
            col(z, 'fox_q'), col(z, 'fox_k'), col(z, 'fox_v'), col(z, 'fox_f'), fox_bf[l], pos,
            fk_p, fv_p, fl_p, fox_heads)
        y = jnp.concatenate([ya, yb, yc, yd], axis=-1)
        return y, (dk, dv, fk, fv, fl, shift_n, wkv_n, conv_n, lru_hn)

    new_p, new_s = [], []
    for l in range(depth):
        w_in_l = jnp.pad(w_in[l], ((0, 0), (0, n_in_pad - n_in))).astype(BF16)
        hn = _rmsnorm(x_all, norm_mix[l]).astype(BF16)
        z_all = _mm(hn, w_in_l)
        z_p = z_all[:n_p, :n_in].reshape(bp, sp, n_in)
        z_s = z_all[n_p:n_tok, :n_in].reshape(bs, ss, n_in)

        prompt_state = (jnp.zeros((bp, rproj), F32), jnp.zeros((bp, G // RWKV_HS, RWKV_HS, RWKV_HS), F32),
                        jnp.zeros((bp, cw - 1, G), F32), jnp.zeros((bp, G), F32))
        y_p, st_p = mixers(z_p, l, pos_p, *prompt_state, (None,) * 5)
        past = (_gather_pages(cache_diff_k, l, page_table), _gather_pages(cache_diff_v, l, page_table),
                _gather_pages(cache_fox_k, l, page_table), _gather_pages(cache_fox_v, l, page_table),
                _gather_pages(cache_fox_logf, l, page_table))
        y_s, st_s = mixers(z_s, l, pos_s, state_rwkv_shift[l], state_rwkv_wkv[l], state_lru_conv[l],
                           state_lru_h[l], past)
        new_p.append(st_p)
        new_s.append(st_s)

        y_all = jnp.concatenate([y_p.reshape(n_p, D), y_s.reshape(n_s, D),
                                 jnp.zeros((m_all - n_tok, D), F32)], axis=0).astype(BF16)
        x_all = _mm(y_all, w_out[l].astype(BF16), resid=x_all)

        hf32 = _rmsnorm(x_all, norm_ffn[l])
        hf = hf32.astype(BF16)
        i = l // 2
        if l % 2 == 0:
            hmid = _gateup(hf, ffn_w_gate[i].astype(BF16), ffn_w_up[i].astype(BF16))
            x_all = _down(hmid, ffn_w_down[i].astype(BF16), x_all)
        else:
            blk = 512 if n_tok * TOP_K >= 4096 else 128
            slot_tok, slot_gate, block_e, n_used, pos = _moe_route(
                hf32, n_tok, moe_router[i], moe_router_b[i], n_experts, blk)
            hf_pad = jnp.concatenate([hf[:n_tok], jnp.zeros((1, D), BF16)], axis=0)
            xs = hf_pad[slot_tok]
            hmid = _moe_gateup(block_e, n_used, xs, moe_w_gate[i].astype(BF16), moe_w_up[i].astype(BF16), blk)
            ys = _moe_down(block_e, n_used, hmid, moe_w_down[i].astype(BF16), slot_gate[:, None], blk)
            y_tok = ys[pos[:, 0]] + ys[pos[:, 1]]
            x_all = x_all + jnp.concatenate([y_tok, jnp.zeros((m_all - n_tok, D), F32)], axis=0)

    y_all = _rmsnorm(x_all, norm_final)
    y_prompt = y_all[:n_p].reshape(bp, sp, D)
    y_sample = y_all[n_p:n_tok].reshape(bs, ss, D)
    outs_p = [jnp.stack([s[i] for s in new_p]) for i in range(9)]
    outs_s = [jnp.stack([s[i] for s in new_s]) for i in range(9)]
    return (y_prompt, y_sample, *outs_p, *outs_s)
```

```python
import functools
import math

import jax
import jax.numpy as jnp
from jax import lax
from jax.experimental import pallas as pl
from jax.experimental.pallas import tpu as pltpu

F32 = jnp.float32
BF16 = jnp.bfloat16

RWKV_HS = 64
RWKV_GN_EPS = 64e-5
LRU_C = 8.0
ROPE_THETA = 500000.0
NORM_EPS = 1e-5
TOP_K = 2
Q_BLOCK = 128

V7X_VMEM_LIMIT_BYTES = 52 * 1024 * 1024
LANE = 128


def _pick(n, candidates):
    for c in candidates:
        if n % c == 0:
            return c
    return n


def _cparams(sem):
    return pltpu.CompilerParams(dimension_semantics=sem, vmem_limit_bytes=V7X_VMEM_LIMIT_BYTES)


def _mm_kernel(x_ref, w_ref, o_ref):
    o_ref[...] = jnp.dot(x_ref[...], w_ref[...], preferred_element_type=F32).astype(o_ref.dtype)


def _mm_res_kernel(x_ref, w_ref, r_ref, o_ref):
    o_ref[...] = r_ref[...] + jnp.dot(x_ref[...], w_ref[...], preferred_element_type=F32)


def _mm(x, w, resid=None, out_dtype=F32):
    M, K = x.shape
    N = w.shape[1]
    tm = _pick(M, (640, 512, 256, 128))
    tn = _pick(N, (512, 256, 128))
    in_specs = [pl.BlockSpec((tm, K), lambda i, j: (i, 0)),
                pl.BlockSpec((K, tn), lambda i, j: (0, j))]
    args = [x, w]
    body = _mm_kernel
    if resid is not None:
        in_specs.append(pl.BlockSpec((tm, tn), lambda i, j: (i, j)))
        args.append(resid)
        body = _mm_res_kernel
    return pl.pallas_call(
        body,
        out_shape=jax.ShapeDtypeStruct((M, N), out_dtype),
        grid=(M // tm, N // tn),
        in_specs=in_specs,
        out_specs=pl.BlockSpec((tm, tn), lambda i, j: (i, j)),
        compiler_params=_cparams(("parallel", "arbitrary")),
        name="mm_res" if resid is not None else "mm",
    )(*args)


def _gateup_kernel(x_ref, wg_ref, wu_ref, o_ref):
    x = x_ref[...]
    g = jnp.dot(x, wg_ref[...], preferred_element_type=F32)
    u = jnp.dot(x, wu_ref[...], preferred_element_type=F32)
    o_ref[...] = (g * jax.nn.sigmoid(g) * u).astype(o_ref.dtype)


def _gateup(x, wg, wu):
    M, K = x.shape
    F = wg.shape[1]
    tm = _pick(M, (640, 512, 256, 128))
    tn = _pick(F, (512, 256, 128))
    return pl.pallas_call(
        _gateup_kernel,
        out_shape=jax.ShapeDtypeStruct((M, F), BF16),
        grid=(M // tm, F // tn),
        in_specs=[pl.BlockSpec((tm, K), lambda i, j: (i, 0)),
                  pl.BlockSpec((K, tn), lambda i, j: (0, j)),
                  pl.BlockSpec((K, tn), lambda i, j: (0, j))],
        out_specs=pl.BlockSpec((tm, tn), lambda i, j: (i, j)),
        compiler_params=_cparams(("parallel", "arbitrary")),
        name="ffn_gateup",
    )(x, wg, wu)


def _down_kernel(x_ref, w_ref, r_ref, o_ref, acc_ref):
    k = pl.program_id(2)

    @pl.when(k == 0)
    def _():
        acc_ref[...] = r_ref[...]

    acc_ref[...] += jnp.dot(x_ref[...], w_ref[...], preferred_element_type=F32)

    @pl.when(k == pl.num_programs(2) - 1)
    def _():
        o_ref[...] = acc_ref[...]


def _down(h, w, resid):
    M, Fd = h.shape
    N = w.shape[1]
    tm = _pick(M, (640, 512, 256, 128))
    tn = _pick(N, (1024, 512, 256, 128))
    tk = _pick(Fd, (2048, 1024, 512, 256, 128))
    return pl.pallas_call(
        _down_kernel,
        out_shape=jax.ShapeDtypeStruct((M, N), F32),
        grid=(M // tm, N // tn, Fd // tk),
        in_specs=[pl.BlockSpec((tm, tk), lambda i, j, k: (i, k)),
                  pl.BlockSpec((tk, tn), lambda i, j, k: (k, j)),
                  pl.BlockSpec((tm, tn), lambda i, j, k: (i, j))],
        out_specs=pl.BlockSpec((tm, tn), lambda i, j, k: (i, j)),
        scratch_shapes=[pltpu.VMEM((tm, tn), F32)],
        compiler_params=_cparams(("parallel", "parallel", "arbitrary")),
        name="ffn_down",
    )(h, w, resid)


def _moe_gateup_kernel(be_ref, nu_ref, x_ref, wg_ref, wu_ref, o_ref):
    i = pl.program_id(1)

    @pl.when(i < nu_ref[0])
    def _():
        x = x_ref[...]
        g = jnp.dot(x, wg_ref[0], preferred_element_type=F32)
        u = jnp.dot(x, wu_ref[0], preferred_element_type=F32)
        o_ref[...] = (g * jax.nn.sigmoid(g) * u).astype(o_ref.dtype)

    @pl.when(i >= nu_ref[0])
    def _():
        o_ref[...] = jnp.zeros_like(o_ref)


def _moe_gateup(block_e, n_used, xs, wg, wu, blk):
    Mp, K = xs.shape
    F = wg.shape[2]
    nblk = Mp // blk
    tn = _pick(F, (512, 256, 128))
    grid_spec = pltpu.PrefetchScalarGridSpec(
        num_scalar_prefetch=2,
        grid=(F // tn, nblk),
        in_specs=[pl.BlockSpec((blk, K), lambda j, i, be, nu: (i, 0)),
                  pl.BlockSpec((1, K, tn), lambda j, i, be, nu: (be[i], 0, j)),
                  pl.BlockSpec((1, K, tn), lambda j, i, be, nu: (be[i], 0, j))],
        out_specs=pl.BlockSpec((blk, tn), lambda j, i, be, nu: (i, j)),
    )
    return pl.pallas_call(
        _moe_gateup_kernel,
        out_shape=jax.ShapeDtypeStruct((Mp, F), BF16),
        grid_spec=grid_spec,
        compiler_params=_cparams(("arbitrary", "arbitrary")),
        name="moe_gateup",
    )(block_e, n_used, xs, wg, wu)


def _moe_down_kernel(be_ref, nu_ref, h_ref, w_ref, g_ref, o_ref, acc_ref):
    i = pl.program_id(1)
    k = pl.program_id(2)
    last = pl.num_programs(2) - 1
    used = i < nu_ref[0]

    @pl.when(jnp.logical_and(used, k == 0))
    def _():
        acc_ref[...] = jnp.zeros_like(acc_ref)

    @pl.when(used)
    def _():
        acc_ref[...] += jnp.dot(h_ref[...], w_ref[0], preferred_element_type=F32)

    @pl.when(jnp.logical_and(used, k == last))
    def _():
        o_ref[...] = acc_ref[...] * g_ref[...]

    @pl.when(jnp.logical_and(jnp.logical_not(used), k == last))
    def _():
        o_ref[...] = jnp.zeros_like(o_ref)


def _moe_down(block_e, n_used, hs, wd, slot_gate, blk):
    Mp, Fd = hs.shape
    N = wd.shape[2]
    nblk = Mp // blk
    tn = _pick(N, (2048, 1024, 512, 256, 128))
    tk = _pick(Fd, (2048, 1024, 512, 256, 128))
    grid_spec = pltpu.PrefetchScalarGridSpec(
        num_scalar_prefetch=2,
        grid=(N // tn, nblk, Fd // tk),
        in_specs=[pl.BlockSpec((blk, tk), lambda j, i, k, be, nu: (i, k)),
                  pl.BlockSpec((1, tk, tn), lambda j, i, k, be, nu: (be[i], k, j)),
                  pl.BlockSpec((blk, 1), lambda j, i, k, be, nu: (i, 0))],
        out_specs=pl.BlockSpec((blk, tn), lambda j, i, k, be, nu: (i, j)),
        scratch_shapes=[pltpu.VMEM((blk, tn), F32)],
    )
    return pl.pallas_call(
        _moe_down_kernel,
        out_shape=jax.ShapeDtypeStruct((Mp, N), F32),
        grid_spec=grid_spec,
        compiler_params=_cparams(("arbitrary", "arbitrary", "arbitrary")),
        name="moe_down",
    )(block_e, n_used, hs, wd, slot_gate)


SCAN_ACCS = 4


def _tree_sum(parts):
    parts = [p for p in parts if p is not None]
    while len(parts) > 1:
        parts = [parts[i] + parts[i + 1] if i + 1 < len(parts) else parts[i] for i in range(0, len(parts), 2)]
    return parts[0]


def _rwkv_scan_kernel(w_ref, kk_ref, b_ref, k_ref, r_ref, v_ref, s0_ref, o_ref, st_ref, s_ref, *, nj, tc):
    c = pl.program_id(0)

    @pl.when(c == 0)
    def _():
        s_ref[...] = s0_ref[...]

    def step(t, carry):
        acc = [None] * SCAN_ACCS
        for j in range(nj):
            p = s_ref[j] * kk_ref[t, j:j + 1, :]
            acc[j % SCAN_ACCS] = p if acc[j % SCAN_ACCS] is None else acc[j % SCAN_ACCS] + p
        sa = -_tree_sum(acc)
        v = v_ref[t]
        oacc = [None] * SCAN_ACCS
        for j in range(nj):
            sn = s_ref[j] * w_ref[t, j:j + 1, :] + sa * b_ref[t, j:j + 1, :] + v * k_ref[t, j:j + 1, :]
            s_ref[j] = sn
            q = sn * r_ref[t, j:j + 1, :]
            oacc[j % SCAN_ACCS] = q if oacc[j % SCAN_ACCS] is None else oacc[j % SCAN_ACCS] + q
        o_ref[t] = _tree_sum(oacc)
        return carry

    lax.fori_loop(0, tc, step, 0)

    @pl.when(c == pl.num_programs(0) - 1)
    def _():
        st_ref[...] = s_ref[...]


def _rwkv_scan(r, w, k, v, kk, a, wkv0):
    B, S, H, N = r.shape
    bh = B * H
    assert LANE % bh == 0 and N % (LANE // bh) == 0, (B, H, N)
    rep = LANE // bh
    rows = N // rep

    def jvec(x):
        x = jnp.transpose(x, (1, 3, 0, 2)).reshape(S, N, bh)
        return jnp.tile(x, (1, 1, rep))

    def ivec(x):
        return jnp.transpose(x, (1, 3, 0, 2)).reshape(S, rows, LANE)

    s0 = jnp.transpose(wkv0.astype(F32), (3, 2, 0, 1)).reshape(N, rows, LANE)
    tc = _pick(S, (64, 32, 16, 8, 4, 2, 1))
    jspec = pl.BlockSpec((tc, N, LANE), lambda c: (c, 0, 0))
    ispec = pl.BlockSpec((tc, rows, LANE), lambda c: (c, 0, 0))
    sspec = pl.BlockSpec((N, rows, LANE), lambda c: (0, 0, 0))
    out, st = pl.pallas_call(
        functools.partial(_rwkv_scan_kernel, nj=N, tc=tc),
        out_shape=(jax.ShapeDtypeStruct((S, rows, LANE), F32), jax.ShapeDtypeStruct((N, rows, LANE), F32)),
        grid=(S // tc,),
        in_specs=[jspec, jspec, jspec, jspec, jspec, ispec, sspec],
        out_specs=(ispec, sspec),
        scratch_shapes=[pltpu.VMEM((N, rows, LANE), F32)],
        compiler_params=_cparams(("arbitrary",)),
        name="rwkv_scan",
    )(jvec(w), jvec(kk), jvec(kk * a), jvec(k), jvec(r), ivec(v), s0)
    out = jnp.transpose(out.reshape(S, N, B, H), (2, 0, 3, 1))
    st = jnp.transpose(st.reshape(N, N, B, H), (2, 3, 1, 0))
    return out, st


def _lru_scan_kernel(a_ref, b_ref, h0_ref, o_ref, h_ref, *, tc):
    @pl.when(pl.program_id(0) == 0)
    def _():
        h_ref[...] = h0_ref[...]

    def step(t, h):
        h = a_ref[t] * h + b_ref[t]
        o_ref[t] = h
        return h

    h_ref[...] = lax.fori_loop(0, tc, step, h_ref[...])


def _lru_scan(a, b, h0):
    B, S, C = a.shape
    assert (B * C) % LANE == 0
    rows = B * C // LANE

    def tmajor(x):
        return jnp.transpose(x, (1, 0, 2)).reshape(S, rows, LANE)

    tc = _pick(S, (128, 64, 32, 16, 8, 4, 2, 1))
    spec = pl.BlockSpec((tc, rows, LANE), lambda c: (c, 0, 0))
    h = pl.pallas_call(
        functools.partial(_lru_scan_kernel, tc=tc),
        out_shape=jax.ShapeDtypeStruct((S, rows, LANE), F32),
        grid=(S // tc,),
        in_specs=[spec, spec, pl.BlockSpec((rows, LANE), lambda c: (0, 0))],
        out_specs=spec,
        scratch_shapes=[pltpu.VMEM((rows, LANE), F32)],
        compiler_params=_cparams(("arbitrary",)),
        name="lru_scan",
    )(tmajor(a), tmajor(b), h0.astype(F32).reshape(rows, LANE))
    return jnp.transpose(h.reshape(S, B, C), (1, 0, 2))


def _rmsnorm(x, g, eps=NORM_EPS):
    xf = x.astype(F32)
    y = xf * lax.rsqrt(jnp.mean(xf * xf, axis=-1, keepdims=True) + eps)
    return y * g.astype(F32)


def _apply_partial_rope(x, pos, rope_dims):
    half = rope_dims // 2
    inv = ROPE_THETA ** (-jnp.arange(half, dtype=F32) * (2.0 / rope_dims))
    ang = pos.astype(F32)[:, None] * inv[None, :]
    cos = jnp.cos(ang)[None, :, None, None, :]
    sin = jnp.sin(ang)[None, :, None, None, :]
    x1, x2 = x[..., :half], x[..., half:rope_dims]
    return jnp.concatenate([x1 * cos - x2 * sin, x2 * cos + x1 * sin, x[..., rope_dims:]], axis=-1)


def _causal_attention(q, k, v, q_pos, k_pos, cum_q=None, cum_k=None):
    B, Sq, H, C, d = q.shape
    blk = Q_BLOCK if Sq % Q_BLOCK == 0 else Sq
    nb = Sq // blk
    scale = d ** -0.5
    use_decay = cum_q is not None
    ck_t = jnp.swapaxes(cum_k, 1, 2) if use_decay else None

    def one_block(args):
        qb, pb = args[0], args[1]
        s = jnp.einsum('bqhcd,bkhcd->bhcqk', qb, k) * scale
        if use_decay:
            cb = jnp.swapaxes(args[2], 1, 2)
            s = s + (cb[:, :, None, :, None] - ck_t[:, :, None, None, :])
        mask = k_pos[None, :] <= pb[:, None]
        s = jnp.where(mask, s, -jnp.inf)
        p = jax.nn.softmax(s, axis=-1)
        return jnp.einsum('bhcqk,bkhv->bqhcv', p, v)

    qs = jnp.moveaxis(q.reshape(B, nb, blk, H, C, d), 1, 0)
    ps = q_pos.reshape(nb, blk)
    if use_decay:
        xs = (qs, ps, jnp.moveaxis(cum_q.reshape(B, nb, blk, H), 1, 0))
    else:
        xs = (qs, ps)
    out = lax.map(one_block, xs)
    return jnp.moveaxis(out, 0, 1).reshape(B, Sq, H, C, -1)


def _rwkv7_time_mix(zr, shift0, wkv0, mu, w0, w2, a0, a2, g2, k_k, k_a, r_k, ln_w, ln_b):
    B, S, _ = zr.shape
    G = w0.shape[0]
    N = RWKV_HS
    H = G // N
    dl, al = w2.shape[0], a2.shape[0]
    prev = jnp.concatenate([shift0[:, None], zr[:, :-1]], axis=1)
    zs = zr + (prev - zr) * mu
    o1 = 3 * G
    o2 = o1 + dl
    o3 = o2 + al
    r, k, v = zs[..., :G], zs[..., G:2 * G], zs[..., 2 * G:3 * G]
    w_raw = -jax.nn.softplus(-(w0 + jnp.tanh(zs[..., o1:o2]) @ w2)) - 0.5
    decay = jnp.exp(-jnp.exp(w_raw))
    a = jax.nn.sigmoid(a0 + zs[..., o2:o3] @ a2)
    g = jax.nn.sigmoid(zs[..., o3:]) @ g2
    kk = (k * k_k).reshape(B, S, H, N)
    kk = kk / jnp.maximum(jnp.linalg.norm(kk, axis=-1, keepdims=True), 1e-12)
    k = (k * (1.0 + (a - 1.0) * k_a)).reshape(B, S, H, N)
    r = r.reshape(B, S, H, N)
    v = v.reshape(B, S, H, N)
    decay = decay.reshape(B, S, H, N)
    a = a.reshape(B, S, H, N)

    out, wkv_n = _rwkv_scan(r, decay, k, v, kk, a, wkv0)
    mean = jnp.mean(out, axis=-1, keepdims=True)
    var = jnp.mean(jnp.square(out - mean), axis=-1, keepdims=True)
    out = (out - mean) * lax.rsqrt(var + RWKV_GN_EPS) * ln_w.reshape(H, N) + ln_b.reshape(H, N)
    out = out + jnp.sum(r * k * r_k, axis=-1, keepdims=True) * v
    y = out.reshape(B, S, G) * g
    return y, zr[:, -1], wkv_n


def _rg_lru_block(zx, zgate, conv0, h0, conv_w, conv_b, wa, ba, wx, bx, lam):
    B, S, C = zx.shape
    cw = conv_w.shape[0]
    nblocks, bs = wa.shape[0], wa.shape[1]
    xpad = jnp.concatenate([conv0, zx], axis=1)
    xc = lax.conv_general_dilated(xpad, conv_w[:, None, :], window_strides=(1,),
                                  padding='VALID', dimension_numbers=('NWC', 'WIO', 'NWC'),
                                  feature_group_count=C) + conv_b
    xb = xc.reshape(B, S, nblocks, bs)
    r = jax.nn.sigmoid(jnp.einsum('bsnc,ncd->bsnd', xb, wa).reshape(B, S, C) + ba)
    i = jax.nn.sigmoid(jnp.einsum('bsnc,ncd->bsnd', xb, wx).reshape(B, S, C) + bx)
    log_a = -LRU_C * r * jax.nn.softplus(-lam)
    a = jnp.exp(log_a)
    b = jnp.sqrt(-jnp.expm1(2.0 * log_a)) * (i * xc)
    h = _lru_scan(a, b, h0)
    y = h * jax.nn.gelu(zgate)
    return y, xpad[:, -(cw - 1):], h[:, -1]


def _differential_attention(zq, zk, zv, pos, k_past, v_past, lq1, lk1, lq2, lk2, subln_g, layer, heads):
    B, S, G = zq.shape
    dh = G // (2 * heads)
    vd = 2 * dh
    rope_dims = dh // 4
    q = _apply_partial_rope(zq.reshape(B, S, heads, 2, dh), pos, rope_dims)
    k = _apply_partial_rope(zk.reshape(B, S, heads, 2, dh), pos, rope_dims)
    v = zv.reshape(B, S, heads, vd)
    k_all, v_all = k, v
    if k_past is not None:
        P = k_past.shape[1]
        k_all = jnp.concatenate([k_past.reshape(B, P, heads, 2, dh), k], axis=1)
        v_all = jnp.concatenate([v_past, v], axis=1)
    k_pos = jnp.arange(k_all.shape[1], dtype=jnp.int32)
    o = _causal_attention(q, k_all, v_all, pos, k_pos)
    lam_init = 0.8 - 0.6 * math.exp(-0.3 * layer)
    lam = jnp.exp(jnp.sum(lq1 * lk1)) - jnp.exp(jnp.sum(lq2 * lk2)) + lam_init
    o = o[..., 0, :] - lam * o[..., 1, :]
    o = _rmsnorm(o, subln_g) * (1.0 - lam_init)
    return o.reshape(B, S, G), k.reshape(B, S, heads, 2 * dh), v


def _forgetting_attention(zq, zk, zv, zf, b_f, pos, k_past, v_past, logf_past, heads):
    B, S, G = zq.shape
    dh = G // heads
    q = zq.reshape(B, S, heads, 1, dh)
    k = zk.reshape(B, S, heads, dh)
    v = zv.reshape(B, S, heads, dh)
    logf = jax.nn.log_sigmoid(zf + b_f)
    k_all, v_all, lf_all = k, v, logf
    if k_past is not None:
        k_all = jnp.concatenate([k_past, k], axis=1)
        v_all = jnp.concatenate([v_past, v], axis=1)
        lf_all = jnp.concatenate([logf_past, logf], axis=1)
    cum = jnp.cumsum(lf_all, axis=1)
    k_pos = jnp.arange(k_all.shape[1], dtype=jnp.int32)
    o = _causal_attention(q, k_all[:, :, :, None], v_all, pos, k_pos, cum[:, -S:], cum)
    return o[:, :, :, 0].reshape(B, S, G), k, v, logf


def _gather_pages(pool, layer, page_table):
    g = pool[layer, page_table]
    return g.reshape((g.shape[0], g.shape[1] * g.shape[2]) + g.shape[3:])


def _moe_route(xn, n, w_router, b_router, n_experts, blk):
    logits = jnp.dot(xn[:n], w_router, precision=lax.Precision.HIGHEST) + b_router
    top_val, top_idx = lax.top_k(logits, TOP_K)
    gates = jax.nn.softmax(top_val, axis=-1)
    m = n * TOP_K
    nblk = -(-m // blk) + n_experts
    e_flat = top_idx.reshape(-1)
    tok = jnp.arange(m, dtype=jnp.int32) // TOP_K
    order = jnp.argsort(e_flat)
    e_sorted = e_flat[order]
    counts = jnp.bincount(e_flat, length=n_experts)
    pcounts = ((counts + blk - 1) // blk) * blk
    pend = jnp.cumsum(pcounts)
    pstart = pend - pcounts
    start = jnp.cumsum(counts) - counts
    dest = (pstart[e_sorted] + (jnp.arange(m) - start[e_sorted])).astype(jnp.int32)
    slot_tok = jnp.full((nblk * blk,), n, jnp.int32).at[dest].set(tok[order])
    slot_gate = jnp.zeros((nblk * blk,), F32).at[dest].set(gates.reshape(-1)[order])
    block_e = jnp.minimum(jnp.searchsorted(pend, jnp.arange(nblk) * blk, side='right'),
                          n_experts - 1).astype(jnp.int32)
    pos = jnp.zeros((m,), jnp.int32).at[order].set(dest).reshape(n, TOP_K)
    n_used = (pend[-1] // blk).astype(jnp.int32).reshape(1)
    return slot_tok, slot_gate, block_e, n_used, pos


def kernel(x_prompt, x_sample, cache_diff_k, cache_diff_v, cache_fox_k, cache_fox_v, cache_fox_logf, state_rwkv_shift, state_rwkv_wkv, state_lru_conv, state_lru_h, page_table, norm_mix, norm_ffn, norm_final, w_in, w_out, rwkv_mu, rwkv_w0, rwkv_w2, rwkv_a0, rwkv_a2, rwkv_g2, rwkv_k_k, rwkv_k_a, rwkv_r_k, rwkv_ln_w, rwkv_ln_b, lru_conv_w, lru_conv_b, lru_wa, lru_ba, lru_wx, lru_bx, lru_lambda, diff_lq1, diff_lk1, diff_lq2, diff_lk2, diff_subln, fox_bf, ffn_w_gate, ffn_w_up, ffn_w_down, moe_router, moe_router_b, moe_w_gate, moe_w_up, moe_w_down):
    bp, sp, D = x_prompt.shape
    bs, ss, _ = x_sample.shape
    depth = w_in.shape[0]
    n_in = w_in.shape[2]
    G = D // 4
    rproj = rwkv_mu.shape[1]
    diff_heads = cache_diff_k.shape[3]
    fox_heads = cache_fox_k.shape[3]
    n_experts = moe_w_gate.shape[1]
    cw = lru_conv_w.shape[1]
    past_len = page_table.shape[1] * cache_diff_k.shape[2]

    n_p = bp * sp
    n_s = bs * ss
    n_tok = n_p + n_s
    m_all = -(-n_tok // LANE) * LANE
    n_in_pad = -(-n_in // 512) * 512

    x_all = jnp.concatenate([x_prompt.reshape(n_p, D), x_sample.reshape(n_s, D),
                             jnp.zeros((m_all - n_tok, D), F32)], axis=0)

    pos_p = jnp.arange(sp, dtype=jnp.int32)
    pos_s = past_len + jnp.arange(ss, dtype=jnp.int32)

    offs = {}
    off = 0
    for name, width in (('rwkv', rproj), ('lru_x', G), ('lru_gate', G), ('diff_q', G), ('diff_k', G),
                        ('diff_v', G), ('fox_q', G), ('fox_k', G), ('fox_v', G), ('fox_f', fox_heads)):
        offs[name] = (off, off + width)
        off += width

    def col(z, name):
        a, b = offs[name]
        return z[..., a:b]

    def mixers(z, l, pos, shift0, wkv0, conv0, lru_h0, past):
        ya, shift_n, wkv_n = _rwkv7_time_mix(
            col(z, 'rwkv'), shift0, wkv0, rwkv_mu[l], rwkv_w0[l], rwkv_w2[l], rwkv_a0[l], rwkv_a2[l],
            rwkv_g2[l], rwkv_k_k[l], rwkv_k_a[l], rwkv_r_k[l], rwkv_ln_w[l], rwkv_ln_b[l])
        yb, conv_n, lru_hn = _rg_lru_block(
            col(z, 'lru_x'), col(z, 'lru_gate'), conv0, lru_h0, lru_conv_w[l], lru_conv_b[l],
            lru_wa[l], lru_ba[l], lru_wx[l], lru_bx[l], lru_lambda[l])
        dk_p, dv_p, fk_p, fv_p, fl_p = past
        yc, dk, dv = _differential_attention(
            col(z, 'diff_q'), col(z, 'diff_k'), col(z, 'diff_v'), pos, dk_p, dv_p,
            diff_lq1[l], diff_lk1[l], diff_lq2[l], diff_lk2[l], diff_subln[l], l, diff_heads)
        yd, fk, fv, fl = _forgetting_attention(
            col(z, 'fox_q'), col(z, 'fox_k'), col(z, 'fox_v'), col(z, 'fox_f'), fox_bf[l], pos,
            fk_p, fv_p, fl_p, fox_heads)
        y = jnp.concatenate([ya, yb, yc, yd], axis=-1)
        return y, (dk, dv, fk, fv, fl, shift_n, wkv_n, conv_n, lru_hn)

    new_p, new_s = [], []
    for l in range(depth):
        w_in_l = jnp.pad(w_in[l], ((0, 0), (0, n_in_pad - n_in))).astype(BF16)
        hn = _rmsnorm(x_all, norm_mix[l]).astype(BF16)
        z_all = _mm(hn, w_in_l)
        z_p = z_all[:n_p, :n_in].reshape(bp, sp, n_in)
        z_s = z_all[n_p:n_tok, :n_in].reshape(bs, ss, n_in)

        prompt_state = (jnp.zeros((bp, rproj), F32), jnp.zeros((bp, G // RWKV_HS, RWKV_HS, RWKV_HS), F32),
                        jnp.zeros((bp, cw - 1, G), F32), jnp.zeros((bp, G), F32))
        y_p, st_p = mixers(z_p, l, pos_p, *prompt_state, (None,) * 5)
        past = (_gather_pages(cache_diff_k, l, page_table), _gather_pages(cache_diff_v, l, page_table),
                _gather_pages(cache_fox_k, l, page_table), _gather_pages(cache_fox_v, l, page_table),
                _gather_pages(cache_fox_logf, l, page_table))
        y_s, st_s = mixers(z_s, l, pos_s, state_rwkv_shift[l], state_rwkv_wkv[l], state_lru_conv[l],
                           state_lru_h[l], past)
        new_p.append(st_p)
        new_s.append(st_s)

        y_all = jnp.concatenate([y_p.reshape(n_p, D), y_s.reshape(n_s, D),
                                 jnp.zeros((m_all - n_tok, D), F32)], axis=0).astype(BF16)
        x_all = _mm(y_all, w_out[l].astype(BF16), resid=x_all)

        hf32 = _rmsnorm(x_all, norm_ffn[l])
        hf = hf32.astype(BF16)
        i = l // 2
        if l % 2 == 0:
            hmid = _gateup(hf, ffn_w_gate[i].astype(BF16), ffn_w_up[i].astype(BF16))
            x_all = _down(hmid, ffn_w_down[i].astype(BF16), x_all)
        else:
            blk = 512 if n_tok * TOP_K >= 4096 else 128
            slot_tok, slot_gate, block_e, n_used, pos = _moe_route(
                hf32, n_tok, moe_router[i], moe_router_b[i], n_experts, blk)
            hf_pad = jnp.concatenate([hf[:n_tok], jnp.zeros((1, D), BF16)], axis=0)
            xs = hf_pad[slot_tok]
            hmid = _moe_gateup(block_e, n_used, xs, moe_w_gate[i].astype(BF16), moe_w_up[i].astype(BF16), blk)
            ys = _moe_down(block_e, n_used, hmid, moe_w_down[i].astype(BF16), slot_gate[:, None], blk)
            y_tok = ys[pos[:, 0]] + ys[pos[:, 1]]
            x_all = x_all + jnp.concatenate([y_tok, jnp.zeros((m_all - n_tok, D), F32)], axis=0)

    y_all = _rmsnorm(x_all, norm_final)
    y_prompt = y_all[:n_p].reshape(bp, sp, D)
    y_sample = y_all[n_p:n_tok].reshape(bs, ss, D)
    outs_p = [jnp.stack([s[i] for s in new_p]) for i in range(9)]
    outs_s = [jnp.stack([s[i] for s in new_s]) for i in range(9)]
    return (y_prompt, y_sample, *outs_p, *outs_s)
```

```python
import functools
import math

import jax
import jax.numpy as jnp
from jax import lax
from jax.experimental import pallas as pl
from jax.experimental.pallas import tpu as pltpu

F32 = jnp.float32
BF16 = jnp.bfloat16

RWKV_HS = 64
RWKV_GN_EPS = 64e-5
LRU_C = 8.0
ROPE_THETA = 500000.0
NORM_EPS = 1e-5
TOP_K = 2
Q_BLOCK = 128

V7X_VMEM_LIMIT_BYTES = 52 * 1024 * 1024
LANE = 128


def _pick(n, candidates):
    for c in candidates:
        if n % c == 0:
            return c
    return n


def _cparams(sem):
    return pltpu.CompilerParams(dimension_semantics=sem, vmem_limit_bytes=V7X_VMEM_LIMIT_BYTES)


def _mm_kernel(x_ref, w_ref, o_ref):
    o_ref[...] = jnp.dot(x_ref[...], w_ref[...], preferred_element_type=F32).astype(o_ref.dtype)


def _mm_res_kernel(x_ref, w_ref, r_ref, o_ref):
    o_ref[...] = r_ref[...] + jnp.dot(x_ref[...], w_ref[...], preferred_element_type=F32)


def _mm(x, w, resid=None, out_dtype=F32):
    M, K = x.shape
    N = w.shape[1]
    tm = _pick(M, (640, 512, 256, 128))
    tn = _pick(N, (512, 256, 128))
    in_specs = [pl.BlockSpec((tm, K), lambda i, j: (i, 0)),
                pl.BlockSpec((K, tn), lambda i, j: (0, j))]
    args = [x, w]
    body = _mm_kernel
    if resid is not None:
        in_specs.append(pl.BlockSpec((tm, tn), lambda i, j: (i, j)))
        args.append(resid)
        body = _mm_res_kernel
    return pl.pallas_call(
        body,
        out_shape=jax.ShapeDtypeStruct((M, N), out_dtype),
        grid=(M // tm, N // tn),
        in_specs=in_specs,
        out_specs=pl.BlockSpec((tm, tn), lambda i, j: (i, j)),
        compiler_params=_cparams(("parallel", "arbitrary")),
        name="mm_res" if resid is not None else "mm",
    )(*args)


def _gateup_kernel(x_ref, wg_ref, wu_ref, o_ref):
    x = x_ref[...]
    g = jnp.dot(x, wg_ref[...], preferred_element_type=F32)
    u = jnp.dot(x, wu_ref[...], preferred_element_type=F32)
    o_ref[...] = (g * jax.nn.sigmoid(g) * u).astype(o_ref.dtype)


def _gateup(x, wg, wu):
    M, K = x.shape
    F = wg.shape[1]
    tm = _pick(M, (640, 512, 256, 128))
    tn = _pick(F, (512, 256, 128))
    return pl.pallas_call(
        _gateup_kernel,
        out_shape=jax.ShapeDtypeStruct((M, F), BF16),
        grid=(M // tm, F // tn),
        in_specs=[pl.BlockSpec((tm, K), lambda i, j: (i, 0)),
                  pl.BlockSpec((K, tn), lambda i, j: (0, j)),
                  pl.BlockSpec((K, tn), lambda i, j: (0, j))],
        out_specs=pl.BlockSpec((tm, tn), lambda i, j: (i, j)),
        compiler_params=_cparams(("parallel", "arbitrary")),
        name="ffn_gateup",
    )(x, wg, wu)


def _down_kernel(x_ref, w_ref, r_ref, o_ref, acc_ref):
    k = pl.program_id(2)

    @pl.when(k == 0)
    def _():
        acc_ref[...] = r_ref[...]

    acc_ref[...] += jnp.dot(x_ref[...], w_ref[...], preferred_element_type=F32)

    @pl.when(k == pl.num_programs(2) - 1)
    def _():
        o_ref[...] = acc_ref[...]


def _down(h, w, resid):
    M, Fd = h.shape
    N = w.shape[1]
    tm = _pick(M, (640, 512, 256, 128))
    tn = _pick(N, (1024, 512, 256, 128))
    tk = _pick(Fd, (2048, 1024, 512, 256, 128))
    return pl.pallas_call(
        _down_kernel,
        out_shape=jax.ShapeDtypeStruct((M, N), F32),
        grid=(M // tm, N // tn, Fd // tk),
        in_specs=[pl.BlockSpec((tm, tk), lambda i, j, k: (i, k)),
                  pl.BlockSpec((tk, tn), lambda i, j, k: (k, j)),
                  pl.BlockSpec((tm, tn), lambda i, j, k: (i, j))],
        out_specs=pl.BlockSpec((tm, tn), lambda i, j, k: (i, j)),
        scratch_shapes=[pltpu.VMEM((tm, tn), F32)],
        compiler_params=_cparams(("parallel", "parallel", "arbitrary")),
        name="ffn_down",
    )(h, w, resid)


def _moe_gateup_kernel(be_ref, nu_ref, x_ref, wg_ref, wu_ref, o_ref):
    i = pl.program_id(1)

    @pl.when(i < nu_ref[0])
    def _():
        x = x_ref[...]
        g = jnp.dot(x, wg_ref[0], preferred_element_type=F32)
        u = jnp.dot(x, wu_ref[0], preferred_element_type=F32)
        o_ref[...] = (g * jax.nn.sigmoid(g) * u).astype(o_ref.dtype)

    @pl.when(i >= nu_ref[0])
    def _():
        o_ref[...] = jnp.zeros_like(o_ref)


def _moe_gateup(block_e, n_used, xs, wg, wu, blk):
    Mp, K = xs.shape
    F = wg.shape[2]
    nblk = Mp // blk
    tn = _pick(F, (512, 256, 128))
    grid_spec = pltpu.PrefetchScalarGridSpec(
        num_scalar_prefetch=2,
        grid=(F // tn, nblk),
        in_specs=[pl.BlockSpec((blk, K), lambda j, i, be, nu: (i, 0)),
                  pl.BlockSpec((1, K, tn), lambda j, i, be, nu: (be[i], 0, j)),
                  pl.BlockSpec((1, K, tn), lambda j, i, be, nu: (be[i], 0, j))],
        out_specs=pl.BlockSpec((blk, tn), lambda j, i, be, nu: (i, j)),
    )
    return pl.pallas_call(
        _moe_gateup_kernel,
        out_shape=jax.ShapeDtypeStruct((Mp, F), BF16),
        grid_spec=grid_spec,
        compiler_params=_cparams(("arbitrary", "arbitrary")),
        name="moe_gateup",
    )(block_e, n_used, xs, wg, wu)


def _moe_down_kernel(be_ref, nu_ref, h_ref, w_ref, g_ref, o_ref, acc_ref):
    i = pl.program_id(1)
    k = pl.program_id(2)
    last = pl.num_programs(2) - 1
    used = i < nu_ref[0]

    @pl.when(jnp.logical_and(used, k == 0))
    def _():
        acc_ref[...] = jnp.zeros_like(acc_ref)

    @pl.when(used)
    def _():
        acc_ref[...] += jnp.dot(h_ref[...], w_ref[0], preferred_element_type=F32)

    @pl.when(jnp.logical_and(used, k == last))
    def _():
        o_ref[...] = acc_ref[...] * g_ref[...]

    @pl.when(jnp.logical_and(jnp.logical_not(used), k == last))
    def _():
        o_ref[...] = jnp.zeros_like(o_ref)


def _moe_down(block_e, n_used, hs, wd, slot_gate, blk):
    Mp, Fd = hs.shape
    N = wd.shape[2]
    nblk = Mp // blk
    tn = _pick(N, (2048, 1024, 512, 256, 128))
    tk = _pick(Fd, (2048, 1024, 512, 256, 128))
    grid_spec = pltpu.PrefetchScalarGridSpec(
        num_scalar_prefetch=2,
        grid=(N // tn, nblk, Fd // tk),
        in_specs=[pl.BlockSpec((blk, tk), lambda j, i, k, be, nu: (i, k)),
                  pl.BlockSpec((1, tk, tn), lambda j, i, k, be, nu: (be[i], k, j)),
                  pl.BlockSpec((blk, 1), lambda j, i, k, be, nu: (i, 0))],
        out_specs=pl.BlockSpec((blk, tn), lambda j, i, k, be, nu: (i, j)),
        scratch_shapes=[pltpu.VMEM((blk, tn), F32)],
    )
    return pl.pallas_call(
        _moe_down_kernel,
        out_shape=jax.ShapeDtypeStruct((Mp, N), F32),
        grid_spec=grid_spec,
        compiler_params=_cparams(("arbitrary", "arbitrary", "arbitrary")),
        name="moe_down",
    )(block_e, n_used, hs, wd, slot_gate)


SCAN_ACCS = 4


def _tree_sum(parts):
    parts = [p for p in parts if p is not None]
    while len(parts) > 1:
        parts = [parts[i] + parts[i + 1] if i + 1 < len(parts) else parts[i] for i in range(0, len(parts), 2)]
    return parts[0]


def _rwkv_scan_kernel(w_ref, kk_ref, b_ref, k_ref, r_ref, v_ref, s0_ref, o_ref, st_ref, s_ref, *, nj, tc):
    c = pl.program_id(0)

    @pl.when(c == 0)
    def _():
        s_ref[...] = s0_ref[...]

    def step(t, carry):
        acc = [None] * SCAN_ACCS
        for j in range(nj):
            p = s_ref[j] * kk_ref[t, j:j + 1, :]
            acc[j % SCAN_ACCS] = p if acc[j % SCAN_ACCS] is None else acc[j % SCAN_ACCS] + p
        sa = -_tree_sum(acc)
        v = v_ref[t]
        oacc = [None] * SCAN_ACCS
        for j in range(nj):
            sn = s_ref[j] * w_ref[t, j:j + 1, :] + sa * b_ref[t, j:j + 1, :] + v * k_ref[t, j:j + 1, :]
            s_ref[j] = sn
            q = sn * r_ref[t, j:j + 1, :]
            oacc[j % SCAN_ACCS] = q if oacc[j % SCAN_ACCS] is None else oacc[j % SCAN_ACCS] + q
        o_ref[t] = _tree_sum(oacc)
        return carry

    lax.fori_loop(0, tc, step, 0)

    @pl.when(c == pl.num_programs(0) - 1)
    def _():
        st_ref[...] = s_ref[...]


def _rwkv_scan(r, w, k, v, kk, a, wkv0):
    B, S, H, N = r.shape
    bh = B * H
    assert LANE % bh == 0 and N % (LANE // bh) == 0, (B, H, N)
    rep = LANE // bh
    rows = N // rep

    def jvec(x):
        x = jnp.transpose(x, (1, 3, 0, 2)).reshape(S, N, bh)
        return jnp.tile(x, (1, 1, rep))

    def ivec(x):
        return jnp.transpose(x, (1, 3, 0, 2)).reshape(S, rows, LANE)

    s0 = jnp.transpose(wkv0.astype(F32), (3, 2, 0, 1)).reshape(N, rows, LANE)
    tc = _pick(S, (64, 32, 16, 8, 4, 2, 1))
    jspec = pl.BlockSpec((tc, N, LANE), lambda c: (c, 0, 0))
    ispec = pl.BlockSpec((tc, rows, LANE), lambda c: (c, 0, 0))
    sspec = pl.BlockSpec((N, rows, LANE), lambda c: (0, 0, 0))
    out, st = pl.pallas_call(
        functools.partial(_rwkv_scan_kernel, nj=N, tc=tc),
        out_shape=(jax.ShapeDtypeStruct((S, rows, LANE), F32), jax.ShapeDtypeStruct((N, rows, LANE), F32)),
        grid=(S // tc,),
        in_specs=[jspec, jspec, jspec, jspec, jspec, ispec, sspec],
        out_specs=(ispec, sspec),
        scratch_shapes=[pltpu.VMEM((N, rows, LANE), F32)],
        compiler_params=_cparams(("arbitrary",)),
        name="rwkv_scan",
    )(jvec(w), jvec(kk), jvec(kk * a), jvec(k), jvec(r), ivec(v), s0)
    out = jnp.transpose(out.reshape(S, N, B, H), (2, 0, 3, 1))
    st = jnp.transpose(st.reshape(N, N, B, H), (2, 3, 1, 0))
    return out, st


def _lru_scan_kernel(a_ref, b_ref, h0_ref, o_ref, h_ref, *, tc):
    @pl.when(pl.program_id(0) == 0)
    def _():
        h_ref[...] = h0_ref[...]

    def step(t, h):
        h = a_ref[t] * h + b_ref[t]
        o_ref[t] = h
        return h

    h_ref[...] = lax.fori_loop(0, tc, step, h_ref[...])


def _lru_scan(a, b, h0):
    B, S, C = a.shape
    assert (B * C) % LANE == 0
    rows = B * C // LANE

    def tmajor(x):
        return jnp.transpose(x, (1, 0, 2)).reshape(S, rows, LANE)

    tc = _pick(S, (128, 64, 32, 16, 8, 4, 2, 1))
    spec = pl.BlockSpec((tc, rows, LANE), lambda c: (c, 0, 0))
    h = pl.pallas_call(
        functools.partial(_lru_scan_kernel, tc=tc),
        out_shape=jax.ShapeDtypeStruct((S, rows, LANE), F32),
        grid=(S // tc,),
        in_specs=[spec, spec, pl.BlockSpec((rows, LANE), lambda c: (0, 0))],
        out_specs=spec,
        scratch_shapes=[pltpu.VMEM((rows, LANE), F32)],
        compiler_params=_cparams(("arbitrary",)),
        name="lru_scan",
    )(tmajor(a), tmajor(b), h0.astype(F32).reshape(rows, LANE))
    return jnp.transpose(h.reshape(S, B, C), (1, 0, 2))


NEG_BIG = -1e30


def _nt_dot(a, b):
    return lax.dot_general(a, b, (((1,), (1,)), ((), ())), preferred_element_type=F32)


def _flash_kernel(q_ref, k_ref, v_ref, *rest, scale, use_bias, tq, tk):
    if use_bias:
        cq_ref, ck_ref, o_ref, m_ref, l_ref, acc_ref = rest
    else:
        o_ref, m_ref, l_ref, acc_ref = rest
    i = pl.program_id(2)
    q = q_ref[...].astype(BF16)
    m_ref[...] = jnp.full_like(m_ref, NEG_BIG)
    l_ref[...] = jnp.zeros_like(l_ref)
    acc_ref[...] = jnp.zeros_like(acc_ref)
    row = i * tq + lax.broadcasted_iota(jnp.int32, (tq, tk), 0)
    col0 = lax.broadcasted_iota(jnp.int32, (tq, tk), 1)

    def body(j, carry):
        start = pl.multiple_of(j * tk, tk)
        ks = k_ref[pl.ds(start, tk), :].astype(BF16)
        vs = v_ref[pl.ds(start, tk), :].astype(BF16)
        s = _nt_dot(q, ks) * scale
        if use_bias:
            s = s + (cq_ref[...] - ck_ref[j])
        s = jnp.where(col0 + j * tk <= row, s, NEG_BIG)
        m_prev = m_ref[:, :1]
        m_new = jnp.maximum(m_prev, jnp.max(s, axis=-1, keepdims=True))
        alpha = jnp.exp(m_prev - m_new)
        p = jnp.exp(s - m_new)
        l_ref[...] = jnp.broadcast_to(alpha * l_ref[:, :1] + jnp.sum(p, axis=-1, keepdims=True), l_ref.shape)
        acc_ref[...] = alpha * acc_ref[...] + jnp.dot(p.astype(BF16), vs, preferred_element_type=F32)
        m_ref[...] = jnp.broadcast_to(m_new, m_ref.shape)
        return carry

    lax.fori_loop(0, i + 1, body, 0)
    o_ref[...] = acc_ref[...] / l_ref[:, :1]


def _flash_attention(q, k, v, nmaps, d, dv, maps_per_v, cum=None):
    B, S, _ = q.shape
    tq = tk = _pick(S, (256, 128, 64, 32, 16, 8))
    nq = S // tq
    use_bias = cum is not None
    in_specs = [pl.BlockSpec((None, tq, d), lambda b, m, i: (b, i, m)),
                pl.BlockSpec((None, S, d), lambda b, m, i: (b, 0, m)),
                pl.BlockSpec((None, S, dv), lambda b, m, i: (b, 0, m // maps_per_v))]
    args = [q, k, v]
    if use_bias:
        cum_t = jnp.transpose(cum, (0, 2, 1))
        in_specs += [pl.BlockSpec((None, None, tq, 1), lambda b, m, i: (b, m, i, 0)),
                     pl.BlockSpec((None, None, nq, 1, tk), lambda b, m, i: (b, m, 0, 0, 0))]
        args += [cum_t[..., None], cum_t.reshape(B, nmaps, nq, 1, tk)]
    return pl.pallas_call(
        functools.partial(_flash_kernel, scale=d ** -0.5, use_bias=use_bias, tq=tq, tk=tk),
        out_shape=jax.ShapeDtypeStruct((B, S, nmaps * dv), F32),
        grid=(B, nmaps, nq),
        in_specs=in_specs,
        out_specs=pl.BlockSpec((None, tq, dv), lambda b, m, i: (b, i, m)),
        scratch_shapes=[pltpu.VMEM((tq, LANE), F32), pltpu.VMEM((tq, LANE), F32), pltpu.VMEM((tq, dv), F32)],
        compiler_params=_cparams(("parallel", "parallel", "arbitrary")),
        name="flash_attn_bias" if use_bias else "flash_attn",
    )(*args)


def _decode_attn_kernel(pt_ref, qbd_ref, kn_ref, vn_ref, *rest, npp, use_bias):
    if use_bias:
        bias_ref, rest = rest[0], rest[1:]
    k_refs, v_refs = rest[:npp], rest[npp:2 * npp]
    o_ref, m_ref, l_ref, acc_ref = rest[2 * npp:]
    p_id = pl.program_id(1)
    qbd = qbd_ref[...]

    @pl.when(p_id == 0)
    def _():
        s_new = jnp.sum(qbd * kn_ref[...], axis=-1, keepdims=True)
        m_ref[...] = jnp.broadcast_to(s_new, m_ref.shape)
        l_ref[...] = jnp.ones_like(l_ref)
        acc_ref[...] = jnp.broadcast_to(vn_ref[...], acc_ref.shape)

    qb = qbd.astype(BF16)
    for u in range(npp):
        s = _nt_dot(qb, k_refs[u][...].astype(BF16))
        if use_bias:
            s = s + bias_ref[u]
        m_prev = m_ref[:, :1]
        m_new = jnp.maximum(m_prev, jnp.max(s, axis=-1, keepdims=True))
        alpha = jnp.exp(m_prev - m_new)
        p = jnp.exp(s - m_new)
        l_ref[...] = jnp.broadcast_to(alpha * l_ref[:, :1] + jnp.sum(p, axis=-1, keepdims=True), l_ref.shape)
        acc_ref[...] = alpha * acc_ref[...] + jnp.dot(p.astype(BF16), v_refs[u][...].astype(BF16),
                                                      preferred_element_type=F32)
        m_ref[...] = jnp.broadcast_to(m_new, m_ref.shape)

    @pl.when(p_id == pl.num_programs(1) - 1)
    def _():
        o_ref[...] = acc_ref[...] / l_ref[:, :1]


def _decode_attention(qbd, k_new, v_new, k_pool, v_pool, layer, page_table, bias=None):
    B, maps, G = qbd.shape
    n_pages = page_table.shape[1]
    page = k_pool.shape[2]
    npp = _pick(n_pages, (4, 2, 1))
    use_bias = bias is not None

    def pool_spec(u):
        return pl.BlockSpec((None, None, page, G),
                            lambda b, p, pt: (layer, pt[b * n_pages + p * npp + u], 0, 0))

    in_specs = [pl.BlockSpec((None, maps, G), lambda b, p, pt: (b, 0, 0)),
                pl.BlockSpec((None, 1, G), lambda b, p, pt: (b, 0, 0)),
                pl.BlockSpec((None, 1, G), lambda b, p, pt: (b, 0, 0))]
    args = [qbd, k_new, v_new]
    if use_bias:
        in_specs.append(pl.BlockSpec((None, npp, maps, page), lambda b, p, pt: (b, p, 0, 0)))
        args.append(bias)
    in_specs += [pool_spec(u) for u in range(npp)] + [pool_spec(u) for u in range(npp)]
    args += [k_pool] * npp + [v_pool] * npp
    grid_spec = pltpu.PrefetchScalarGridSpec(
        num_scalar_prefetch=1,
        grid=(B, n_pages // npp),
        in_specs=in_specs,
        out_specs=pl.BlockSpec((None, maps, G), lambda b, p, pt: (b, 0, 0)),
        scratch_shapes=[pltpu.VMEM((maps, LANE), F32), pltpu.VMEM((maps, LANE), F32), pltpu.VMEM((maps, G), F32)],
    )
    return pl.pallas_call(
        functools.partial(_decode_attn_kernel, npp=npp, use_bias=use_bias),
        out_shape=jax.ShapeDtypeStruct((B, maps, G), F32),
        grid_spec=grid_spec,
        compiler_params=_cparams(("parallel", "arbitrary")),
        name="decode_attn_bias" if use_bias else "decode_attn",
    )(page_table.reshape(-1).astype(jnp.int32), *args)


def _block_diag_queries(q, scale):
    B, maps, d = q.shape
    eye = jnp.eye(maps, dtype=q.dtype)
    return (eye[None, :, :, None] * (q * scale)[:, None, :, :]).reshape(B, maps, maps * d)


def _diag_blocks(o, nblocks):
    B, maps, G = o.shape
    w = G // nblocks
    per = maps // nblocks
    o = o.reshape(B, nblocks, per, nblocks, w)
    idx = jnp.arange(nblocks)
    return o[:, idx, :, idx].transpose(1, 0, 2, 3).reshape(B, maps, w)


def _rmsnorm(x, g, eps=NORM_EPS):
    xf = x.astype(F32)
    y = xf * lax.rsqrt(jnp.mean(xf * xf, axis=-1, keepdims=True) + eps)
    return y * g.astype(F32)


def _apply_partial_rope(x, pos, rope_dims):
    half = rope_dims // 2
    inv = ROPE_THETA ** (-jnp.arange(half, dtype=F32) * (2.0 / rope_dims))
    ang = pos.astype(F32)[:, None] * inv[None, :]
    cos = jnp.cos(ang)[None, :, None, None, :]
    sin = jnp.sin(ang)[None, :, None, None, :]
    x1, x2 = x[..., :half], x[..., half:rope_dims]
    return jnp.concatenate([x1 * cos - x2 * sin, x2 * cos + x1 * sin, x[..., rope_dims:]], axis=-1)


def _rwkv7_time_mix(zr, shift0, wkv0, mu, w0, w2, a0, a2, g2, k_k, k_a, r_k, ln_w, ln_b):
    B, S, _ = zr.shape
    G = w0.shape[0]
    N = RWKV_HS
    H = G // N
    dl, al = w2.shape[0], a2.shape[0]
    prev = jnp.concatenate([shift0[:, None], zr[:, :-1]], axis=1)
    zs = zr + (prev - zr) * mu
    o1 = 3 * G
    o2 = o1 + dl
    o3 = o2 + al
    r, k, v = zs[..., :G], zs[..., G:2 * G], zs[..., 2 * G:3 * G]
    w_raw = -jax.nn.softplus(-(w0 + jnp.tanh(zs[..., o1:o2]) @ w2)) - 0.5
    decay = jnp.exp(-jnp.exp(w_raw))
    a = jax.nn.sigmoid(a0 + zs[..., o2:o3] @ a2)
    g = jax.nn.sigmoid(zs[..., o3:]) @ g2
    kk = (k * k_k).reshape(B, S, H, N)
    kk = kk / jnp.maximum(jnp.linalg.norm(kk, axis=-1, keepdims=True), 1e-12)
    k = (k * (1.0 + (a - 1.0) * k_a)).reshape(B, S, H, N)
    r = r.reshape(B, S, H, N)
    v = v.reshape(B, S, H, N)
    decay = decay.reshape(B, S, H, N)
    a = a.reshape(B, S, H, N)

    out, wkv_n = _rwkv_scan(r, decay, k, v, kk, a, wkv0)
    mean = jnp.mean(out, axis=-1, keepdims=True)
    var = jnp.mean(jnp.square(out - mean), axis=-1, keepdims=True)
    out = (out - mean) * lax.rsqrt(var + RWKV_GN_EPS) * ln_w.reshape(H, N) + ln_b.reshape(H, N)
    out = out + jnp.sum(r * k * r_k, axis=-1, keepdims=True) * v
    y = out.reshape(B, S, G) * g
    return y, zr[:, -1], wkv_n


def _rg_lru_block(zx, zgate, conv0, h0, conv_w, conv_b, wa, ba, wx, bx, lam):
    B, S, C = zx.shape
    cw = conv_w.shape[0]
    nblocks, bs = wa.shape[0], wa.shape[1]
    xpad = jnp.concatenate([conv0, zx], axis=1)
    xc = lax.conv_general_dilated(xpad, conv_w[:, None, :], window_strides=(1,),
                                  padding='VALID', dimension_numbers=('NWC', 'WIO', 'NWC'),
                                  feature_group_count=C) + conv_b
    xb = xc.reshape(B, S, nblocks, bs)
    r = jax.nn.sigmoid(jnp.einsum('bsnc,ncd->bsnd', xb, wa).reshape(B, S, C) + ba)
    i = jax.nn.sigmoid(jnp.einsum('bsnc,ncd->bsnd', xb, wx).reshape(B, S, C) + bx)
    log_a = -LRU_C * r * jax.nn.softplus(-lam)
    a = jnp.exp(log_a)
    b = jnp.sqrt(-jnp.expm1(2.0 * log_a)) * (i * xc)
    h = _lru_scan(a, b, h0)
    y = h * jax.nn.gelu(zgate)
    return y, xpad[:, -(cw - 1):], h[:, -1]


def _differential_attention(zq, zk, zv, pos, lq1, lk1, lq2, lk2, subln_g, layer, heads, paged=None):
    B, S, G = zq.shape
    dh = G // (2 * heads)
    vd = 2 * dh
    rope_dims = dh // 4
    q = _apply_partial_rope(zq.reshape(B, S, heads, 2, dh), pos, rope_dims)
    k = _apply_partial_rope(zk.reshape(B, S, heads, 2, dh), pos, rope_dims)
    v = zv.reshape(B, S, heads, vd)
    if paged is None:
        o = _flash_attention(q.reshape(B, S, G), k.reshape(B, S, G), zv, 2 * heads, dh, vd, 2)
        o = o.reshape(B, S, heads, 2, vd)
    else:
        assert S == 1
        k_pool, v_pool, page_table = paged
        qbd = _block_diag_queries(q.reshape(B, 2 * heads, dh), dh ** -0.5)
        o = _decode_attention(qbd, k.reshape(B, 1, G), zv, k_pool.reshape(k_pool.shape[:3] + (G,)),
                              v_pool.reshape(v_pool.shape[:3] + (G,)), layer, page_table)
        o = _diag_blocks(o, heads).reshape(B, 1, heads, 2, vd)
    lam_init = 0.8 - 0.6 * math.exp(-0.3 * layer)
    lam = jnp.exp(jnp.sum(lq1 * lk1)) - jnp.exp(jnp.sum(lq2 * lk2)) + lam_init
    o = o[..., 0, :] - lam * o[..., 1, :]
    o = _rmsnorm(o, subln_g) * (1.0 - lam_init)
    return o.reshape(B, S, G), k.reshape(B, S, heads, 2 * dh), v


def _forgetting_attention(zq, zk, zv, zf, b_f, layer, heads, paged=None):
    B, S, G = zq.shape
    dh = G // heads
    k = zk.reshape(B, S, heads, dh)
    v = zv.reshape(B, S, heads, dh)
    logf = jax.nn.log_sigmoid(zf + b_f)
    if paged is None:
        cum = jnp.cumsum(logf, axis=1)
        o = _flash_attention(zq, zk, zv, heads, dh, dh, 1, cum=cum)
    else:
        assert S == 1
        k_pool, v_pool, logf_pool, page_table = paged
        n_pages, page = page_table.shape[1], k_pool.shape[2]
        lf_all = jnp.concatenate([_gather_pages(logf_pool, layer, page_table), logf], axis=1)
        cum = jnp.cumsum(lf_all, axis=1)
        bias = (cum[:, -1:] - cum[:, :-1]).reshape(B, n_pages, page, heads).transpose(0, 1, 3, 2)
        qbd = _block_diag_queries(zq.reshape(B, heads, dh), dh ** -0.5)
        o = _decode_attention(qbd, zk, zv, k_pool.reshape(k_pool.shape[:3] + (G,)),
                              v_pool.reshape(v_pool.shape[:3] + (G,)), layer, page_table, bias=bias)
        o = _diag_blocks(o, heads).reshape(B, 1, G)
    return o, k, v, logf


def _gather_pages(pool, layer, page_table):
    g = pool[layer, page_table]
    return g.reshape((g.shape[0], g.shape[1] * g.shape[2]) + g.shape[3:])


def _moe_route(xn, n, w_router, b_router, n_experts, blk):
    logits = jnp.dot(xn[:n], w_router, precision=lax.Precision.HIGHEST) + b_router
    top_val, top_idx = lax.top_k(logits, TOP_K)
    gates = jax.nn.softmax(top_val, axis=-1)
    m = n * TOP_K
    nblk = -(-m // blk) + n_experts
    e_flat = top_idx.reshape(-1)
    tok = jnp.arange(m, dtype=jnp.int32) // TOP_K
    order = jnp.argsort(e_flat)
    e_sorted = e_flat[order]
    counts = jnp.bincount(e_flat, length=n_experts)
    pcounts = ((counts + blk - 1) // blk) * blk
    pend = jnp.cumsum(pcounts)
    pstart = pend - pcounts
    start = jnp.cumsum(counts) - counts
    dest = (pstart[e_sorted] + (jnp.arange(m) - start[e_sorted])).astype(jnp.int32)
    slot_tok = jnp.full((nblk * blk,), n, jnp.int32).at[dest].set(tok[order])
    slot_gate = jnp.zeros((nblk * blk,), F32).at[dest].set(gates.reshape(-1)[order])
    block_e = jnp.minimum(jnp.searchsorted(pend, jnp.arange(nblk) * blk, side='right'),
                          n_experts - 1).astype(jnp.int32)
    pos = jnp.zeros((m,), jnp.int32).at[order].set(dest).reshape(n, TOP_K)
    n_used = (pend[-1] // blk).astype(jnp.int32).reshape(1)
    return slot_tok, slot_gate, block_e, n_used, pos


def kernel(x_prompt, x_sample, cache_diff_k, cache_diff_v, cache_fox_k, cache_fox_v, cache_fox_logf, state_rwkv_shift, state_rwkv_wkv, state_lru_conv, state_lru_h, page_table, norm_mix, norm_ffn, norm_final, w_in, w_out, rwkv_mu, rwkv_w0, rwkv_w2, rwkv_a0, rwkv_a2, rwkv_g2, rwkv_k_k, rwkv_k_a, rwkv_r_k, rwkv_ln_w, rwkv_ln_b, lru_conv_w, lru_conv_b, lru_wa, lru_ba, lru_wx, lru_bx, lru_lambda, diff_lq1, diff_lk1, diff_lq2, diff_lk2, diff_subln, fox_bf, ffn_w_gate, ffn_w_up, ffn_w_down, moe_router, moe_router_b, moe_w_gate, moe_w_up, moe_w_down):
    bp, sp, D = x_prompt.shape
    bs, ss, _ = x_sample.shape
    depth = w_in.shape[0]
    n_in = w_in.shape[2]
    G = D // 4
    rproj = rwkv_mu.shape[1]
    diff_heads = cache_diff_k.shape[3]
    fox_heads = cache_fox_k.shape[3]
    n_experts = moe_w_gate.shape[1]
    cw = lru_conv_w.shape[1]
    past_len = page_table.shape[1] * cache_diff_k.shape[2]

    n_p = bp * sp
    n_s = bs * ss
    n_tok = n_p + n_s
    m_all = -(-n_tok // LANE) * LANE
    n_in_pad = -(-n_in // 512) * 512

    x_all = jnp.concatenate([x_prompt.reshape(n_p, D), x_sample.reshape(n_s, D),
                             jnp.zeros((m_all - n_tok, D), F32)], axis=0)

    pos_p = jnp.arange(sp, dtype=jnp.int32)
    pos_s = past_len + jnp.arange(ss, dtype=jnp.int32)

    offs = {}
    off = 0
    for name, width in (('rwkv', rproj), ('lru_x', G), ('lru_gate', G), ('diff_q', G), ('diff_k', G),
                        ('diff_v', G), ('fox_q', G), ('fox_k', G), ('fox_v', G), ('fox_f', fox_heads)):
        offs[name] = (off, off + width)
        off += width

    def col(z, name):
        a, b = offs[name]
        return z[..., a:b]

    def mixers(z, l, pos, shift0, wkv0, conv0, lru_h0, sample):
        ya, shift_n, wkv_n = _rwkv7_time_mix(
            col(z, 'rwkv'), shift0, wkv0, rwkv_mu[l], rwkv_w0[l], rwkv_w2[l], rwkv_a0[l], rwkv_a2[l],
            rwkv_g2[l], rwkv_k_k[l], rwkv_k_a[l], rwkv_r_k[l], rwkv_ln_w[l], rwkv_ln_b[l])
        yb, conv_n, lru_hn = _rg_lru_block(
            col(z, 'lru_x'), col(z, 'lru_gate'), conv0, lru_h0, lru_conv_w[l], lru_conv_b[l],
            lru_wa[l], lru_ba[l], lru_wx[l], lru_bx[l], lru_lambda[l])
        yc, dk, dv = _differential_attention(
            col(z, 'diff_q'), col(z, 'diff_k'), col(z, 'diff_v'), pos,
            diff_lq1[l], diff_lk1[l], diff_lq2[l], diff_lk2[l], diff_subln[l], l, diff_heads,
            paged=(cache_diff_k, cache_diff_v, page_table) if sample else None)
        yd, fk, fv, fl = _forgetting_attention(
            col(z, 'fox_q'), col(z, 'fox_k'), col(z, 'fox_v'), col(z, 'fox_f'), fox_bf[l], l, fox_heads,
            paged=(cache_fox_k, cache_fox_v, cache_fox_logf, page_table) if sample else None)
        y = jnp.concatenate([ya, yb, yc, yd], axis=-1)
        return y, (dk, dv, fk, fv, fl, shift_n, wkv_n, conv_n, lru_hn)

    new_p, new_s = [], []
    for l in range(depth):
        w_in_l = jnp.pad(w_in[l], ((0, 0), (0, n_in_pad - n_in))).astype(BF16)
        hn = _rmsnorm(x_all, norm_mix[l]).astype(BF16)
        z_all = _mm(hn, w_in_l)
        z_p = z_all[:n_p, :n_in].reshape(bp, sp, n_in)
        z_s = z_all[n_p:n_tok, :n_in].reshape(bs, ss, n_in)

        prompt_state = (jnp.zeros((bp, rproj), F32), jnp.zeros((bp, G // RWKV_HS, RWKV_HS, RWKV_HS), F32),
                        jnp.zeros((bp, cw - 1, G), F32), jnp.zeros((bp, G), F32))
        y_p, st_p = mixers(z_p, l, pos_p, *prompt_state, False)
        y_s, st_s = mixers(z_s, l, pos_s, state_rwkv_shift[l], state_rwkv_wkv[l], state_lru_conv[l],
                           state_lru_h[l], True)
        new_p.append(st_p)
        new_s.append(st_s)

        y_all = jnp.concatenate([y_p.reshape(n_p, D), y_s.reshape(n_s, D),
                                 jnp.zeros((m_all - n_tok, D), F32)], axis=0).astype(BF16)
        x_all = _mm(y_all, w_out[l].astype(BF16), resid=x_all)

        hf32 = _rmsnorm(x_all, norm_ffn[l])
        hf = hf32.astype(BF16)
        i = l // 2
        if l % 2 == 0:
            hmid = _gateup(hf, ffn_w_gate[i].astype(BF16), ffn_w_up[i].astype(BF16))
            x_all = _down(hmid, ffn_w_down[i].astype(BF16), x_all)
        else:
            blk = 512 if n_tok * TOP_K >= 4096 else 128
            slot_tok, slot_gate, block_e, n_used, pos = _moe_route(
                hf32, n_tok, moe_router[i], moe_router_b[i], n_experts, blk)
            hf_pad = jnp.concatenate([hf[:n_tok], jnp.zeros((1, D), BF16)], axis=0)
            xs = hf_pad[slot_tok]
            hmid = _moe_gateup(block_e, n_used, xs, moe_w_gate[i].astype(BF16), moe_w_up[i].astype(BF16), blk)
            ys = _moe_down(block_e, n_used, hmid, moe_w_down[i].astype(BF16), slot_gate[:, None], blk)
            y_tok = ys[pos[:, 0]] + ys[pos[:, 1]]
            x_all = x_all + jnp.concatenate([y_tok, jnp.zeros((m_all - n_tok, D), F32)], axis=0)

    y_all = _rmsnorm(x_all, norm_final)
    y_prompt = y_all[:n_p].reshape(bp, sp, D)
    y_sample = y_all[n_p:n_tok].reshape(bs, ss, D)
    outs_p = [jnp.stack([s[i] for s in new_p]) for i in range(9)]
    outs_s = [jnp.stack([s[i] for s in new_s]) for i in range(9)]
    return (y_prompt, y_sample, *outs_p, *outs_s)
```

```python
import functools
import math

import jax
import jax.numpy as jnp
from jax import lax
from jax.experimental import pallas as pl
from jax.experimental.pallas import tpu as pltpu

F32 = jnp.float32
BF16 = jnp.bfloat16

RWKV_HS = 64
RWKV_GN_EPS = 64e-5
LRU_C = 8.0
ROPE_THETA = 500000.0
NORM_EPS = 1e-5
TOP_K = 2
Q_BLOCK = 128

V7X_VMEM_LIMIT_BYTES = 52 * 1024 * 1024
LANE = 128


def _pick(n, candidates):
    for c in candidates:
        if n % c == 0:
            return c
    return n


def _cparams(sem):
    return pltpu.CompilerParams(dimension_semantics=sem, vmem_limit_bytes=V7X_VMEM_LIMIT_BYTES)


def _mm_kernel(x_ref, w_ref, o_ref):
    o_ref[...] = jnp.dot(x_ref[...], w_ref[...], preferred_element_type=F32).astype(o_ref.dtype)


def _mm_res_kernel(x_ref, w_ref, r_ref, o_ref):
    o_ref[...] = r_ref[...] + jnp.dot(x_ref[...], w_ref[...], preferred_element_type=F32)


def _mm(x, w, resid=None, out_dtype=F32):
    M, K = x.shape
    N = w.shape[1]
    tm = _pick(M, (640, 512, 256, 128))
    tn = _pick(N, (512, 256, 128))
    in_specs = [pl.BlockSpec((tm, K), lambda i, j: (i, 0)),
                pl.BlockSpec((K, tn), lambda i, j: (0, j))]
    args = [x, w]
    body = _mm_kernel
    if resid is not None:
        in_specs.append(pl.BlockSpec((tm, tn), lambda i, j: (i, j)))
        args.append(resid)
        body = _mm_res_kernel
    return pl.pallas_call(
        body,
        out_shape=jax.ShapeDtypeStruct((M, N), out_dtype),
        grid=(M // tm, N // tn),
        in_specs=in_specs,
        out_specs=pl.BlockSpec((tm, tn), lambda i, j: (i, j)),
        compiler_params=_cparams(("parallel", "arbitrary")),
        name="mm_res" if resid is not None else "mm",
    )(*args)


def _gateup_kernel(x_ref, wg_ref, wu_ref, o_ref):
    x = x_ref[...]
    g = jnp.dot(x, wg_ref[...], preferred_element_type=F32)
    u = jnp.dot(x, wu_ref[...], preferred_element_type=F32)
    o_ref[...] = (g * jax.nn.sigmoid(g) * u).astype(o_ref.dtype)


def _gateup(x, wg, wu):
    M, K = x.shape
    F = wg.shape[1]
    tm = _pick(M, (640, 512, 256, 128))
    tn = _pick(F, (512, 256, 128))
    return pl.pallas_call(
        _gateup_kernel,
        out_shape=jax.ShapeDtypeStruct((M, F), BF16),
        grid=(M // tm, F // tn),
        in_specs=[pl.BlockSpec((tm, K), lambda i, j: (i, 0)),
                  pl.BlockSpec((K, tn), lambda i, j: (0, j)),
                  pl.BlockSpec((K, tn), lambda i, j: (0, j))],
        out_specs=pl.BlockSpec((tm, tn), lambda i, j: (i, j)),
        compiler_params=_cparams(("parallel", "arbitrary")),
        name="ffn_gateup",
    )(x, wg, wu)


def _down_kernel(x_ref, w_ref, r_ref, o_ref, acc_ref):
    k = pl.program_id(2)

    @pl.when(k == 0)
    def _():
        acc_ref[...] = r_ref[...]

    acc_ref[...] += jnp.dot(x_ref[...], w_ref[...], preferred_element_type=F32)

    @pl.when(k == pl.num_programs(2) - 1)
    def _():
        o_ref[...] = acc_ref[...]


def _down(h, w, resid):
    M, Fd = h.shape
    N = w.shape[1]
    tm = _pick(M, (640, 512, 256, 128))
    tn = _pick(N, (1024, 512, 256, 128))
    tk = _pick(Fd, (2048, 1024, 512, 256, 128))
    return pl.pallas_call(
        _down_kernel,
        out_shape=jax.ShapeDtypeStruct((M, N), F32),
        grid=(M // tm, N // tn, Fd // tk),
        in_specs=[pl.BlockSpec((tm, tk), lambda i, j, k: (i, k)),
                  pl.BlockSpec((tk, tn), lambda i, j, k: (k, j)),
                  pl.BlockSpec((tm, tn), lambda i, j, k: (i, j))],
        out_specs=pl.BlockSpec((tm, tn), lambda i, j, k: (i, j)),
        scratch_shapes=[pltpu.VMEM((tm, tn), F32)],
        compiler_params=_cparams(("parallel", "parallel", "arbitrary")),
        name="ffn_down",
    )(h, w, resid)


def _moe_gateup_kernel(be_ref, nu_ref, x_ref, wg_ref, wu_ref, o_ref):
    i = pl.program_id(1)

    @pl.when(i < nu_ref[0])
    def _():
        x = x_ref[...]
        g = jnp.dot(x, wg_ref[0], preferred_element_type=F32)
        u = jnp.dot(x, wu_ref[0], preferred_element_type=F32)
        o_ref[...] = (g * jax.nn.sigmoid(g) * u).astype(o_ref.dtype)

    @pl.when(i >= nu_ref[0])
    def _():
        o_ref[...] = jnp.zeros_like(o_ref)


def _moe_gateup(block_e, n_used, xs, wg, wu, blk):
    Mp, K = xs.shape
    F = wg.shape[2]
    nblk = Mp // blk
    tn = _pick(F, (512, 256, 128))
    grid_spec = pltpu.PrefetchScalarGridSpec(
        num_scalar_prefetch=2,
        grid=(F // tn, nblk),
        in_specs=[pl.BlockSpec((blk, K), lambda j, i, be, nu: (i, 0)),
                  pl.BlockSpec((1, K, tn), lambda j, i, be, nu: (be[i], 0, j)),
                  pl.BlockSpec((1, K, tn), lambda j, i, be, nu: (be[i], 0, j))],
        out_specs=pl.BlockSpec((blk, tn), lambda j, i, be, nu: (i, j)),
    )
    return pl.pallas_call(
        _moe_gateup_kernel,
        out_shape=jax.ShapeDtypeStruct((Mp, F), BF16),
        grid_spec=grid_spec,
        compiler_params=_cparams(("arbitrary", "arbitrary")),
        name="moe_gateup",
    )(block_e, n_used, xs, wg, wu)


def _moe_down_kernel(be_ref, nu_ref, h_ref, w_ref, g_ref, o_ref, acc_ref):
    i = pl.program_id(1)
    k = pl.program_id(2)
    last = pl.num_programs(2) - 1
    used = i < nu_ref[0]

    @pl.when(jnp.logical_and(used, k == 0))
    def _():
        acc_ref[...] = jnp.zeros_like(acc_ref)

    @pl.when(used)
    def _():
        acc_ref[...] += jnp.dot(h_ref[...], w_ref[0], preferred_element_type=F32)

    @pl.when(jnp.logical_and(used, k == last))
    def _():
        o_ref[...] = acc_ref[...] * g_ref[...]

    @pl.when(jnp.logical_and(jnp.logical_not(used), k == last))
    def _():
        o_ref[...] = jnp.zeros_like(o_ref)


def _moe_down(block_e, n_used, hs, wd, slot_gate, blk):
    Mp, Fd = hs.shape
    N = wd.shape[2]
    nblk = Mp // blk
    tn = _pick(N, (2048, 1024, 512, 256, 128))
    tk = _pick(Fd, (2048, 1024, 512, 256, 128))
    grid_spec = pltpu.PrefetchScalarGridSpec(
        num_scalar_prefetch=2,
        grid=(N // tn, nblk, Fd // tk),
        in_specs=[pl.BlockSpec((blk, tk), lambda j, i, k, be, nu: (i, k)),
                  pl.BlockSpec((1, tk, tn), lambda j, i, k, be, nu: (be[i], k, j)),
                  pl.BlockSpec((blk, 1), lambda j, i, k, be, nu: (i, 0))],
        out_specs=pl.BlockSpec((blk, tn), lambda j, i, k, be, nu: (i, j)),
        scratch_shapes=[pltpu.VMEM((blk, tn), F32)],
    )
    return pl.pallas_call(
        _moe_down_kernel,
        out_shape=jax.ShapeDtypeStruct((Mp, N), F32),
        grid_spec=grid_spec,
        compiler_params=_cparams(("arbitrary", "arbitrary", "arbitrary")),
        name="moe_down",
    )(block_e, n_used, hs, wd, slot_gate)


SCAN_ACCS = 4


def _tree_sum(parts):
    parts = [p for p in parts if p is not None]
    while len(parts) > 1:
        parts = [parts[i] + parts[i + 1] if i + 1 < len(parts) else parts[i] for i in range(0, len(parts), 2)]
    return parts[0]


def _rwkv_scan_kernel(w_ref, kk_ref, b_ref, k_ref, r_ref, v_ref, s0_ref, o_ref, st_ref, s_ref, *, nj, tc):
    c = pl.program_id(0)

    @pl.when(c == 0)
    def _():
        s_ref[...] = s0_ref[...]

    def step(t, carry):
        acc = [None] * SCAN_ACCS
        for j in range(nj):
            p = s_ref[j] * kk_ref[t, j:j + 1, :]
            acc[j % SCAN_ACCS] = p if acc[j % SCAN_ACCS] is None else acc[j % SCAN_ACCS] + p
        sa = -_tree_sum(acc)
        v = v_ref[t]
        oacc = [None] * SCAN_ACCS
        for j in range(nj):
            sn = s_ref[j] * w_ref[t, j:j + 1, :] + sa * b_ref[t, j:j + 1, :] + v * k_ref[t, j:j + 1, :]
            s_ref[j] = sn
            q = sn * r_ref[t, j:j + 1, :]
            oacc[j % SCAN_ACCS] = q if oacc[j % SCAN_ACCS] is None else oacc[j % SCAN_ACCS] + q
        o_ref[t] = _tree_sum(oacc)
        return carry

    lax.fori_loop(0, tc, step, 0)

    @pl.when(c == pl.num_programs(0) - 1)
    def _():
        st_ref[...] = s_ref[...]


def _rwkv_scan(r, w, k, v, kk, a, wkv0):
    B, S, H, N = r.shape
    bh = B * H
    assert LANE % bh == 0 and N % (LANE // bh) == 0, (B, H, N)
    rep = LANE // bh
    rows = N // rep

    def jvec(x):
        x = jnp.transpose(x, (1, 3, 0, 2)).reshape(S, N, bh)
        return jnp.tile(x, (1, 1, rep))

    def ivec(x):
        return jnp.transpose(x, (1, 3, 0, 2)).reshape(S, rows, LANE)

    s0 = jnp.transpose(wkv0.astype(F32), (3, 2, 0, 1)).reshape(N, rows, LANE)
    tc = _pick(S, (64, 32, 16, 8, 4, 2, 1))
    jspec = pl.BlockSpec((tc, N, LANE), lambda c: (c, 0, 0))
    ispec = pl.BlockSpec((tc, rows, LANE), lambda c: (c, 0, 0))
    sspec = pl.BlockSpec((N, rows, LANE), lambda c: (0, 0, 0))
    out, st = pl.pallas_call(
        functools.partial(_rwkv_scan_kernel, nj=N, tc=tc),
        out_shape=(jax.ShapeDtypeStruct((S, rows, LANE), F32), jax.ShapeDtypeStruct((N, rows, LANE), F32)),
        grid=(S // tc,),
        in_specs=[jspec, jspec, jspec, jspec, jspec, ispec, sspec],
        out_specs=(ispec, sspec),
        scratch_shapes=[pltpu.VMEM((N, rows, LANE), F32)],
        compiler_params=_cparams(("arbitrary",)),
        name="rwkv_scan",
    )(jvec(w), jvec(kk), jvec(kk * a), jvec(k), jvec(r), ivec(v), s0)
    out = jnp.transpose(out.reshape(S, N, B, H), (2, 0, 3, 1))
    st = jnp.transpose(st.reshape(N, N, B, H), (2, 3, 1, 0))
    return out, st


def _lru_scan_kernel(a_ref, b_ref, h0_ref, o_ref, h_ref, *, tc):
    @pl.when(pl.program_id(0) == 0)
    def _():
        h_ref[...] = h0_ref[...]

    def step(t, h):
        h = a_ref[t] * h + b_ref[t]
        o_ref[t] = h
        return h

    h_ref[...] = lax.fori_loop(0, tc, step, h_ref[...])


def _lru_scan(a, b, h0):
    B, S, C = a.shape
    assert (B * C) % LANE == 0
    rows = B * C // LANE

    def tmajor(x):
        return jnp.transpose(x, (1, 0, 2)).reshape(S, rows, LANE)

    tc = _pick(S, (128, 64, 32, 16, 8, 4, 2, 1))
    spec = pl.BlockSpec((tc, rows, LANE), lambda c: (c, 0, 0))
    h = pl.pallas_call(
        functools.partial(_lru_scan_kernel, tc=tc),
        out_shape=jax.ShapeDtypeStruct((S, rows, LANE), F32),
        grid=(S // tc,),
        in_specs=[spec, spec, pl.BlockSpec((rows, LANE), lambda c: (0, 0))],
        out_specs=spec,
        scratch_shapes=[pltpu.VMEM((rows, LANE), F32)],
        compiler_params=_cparams(("arbitrary",)),
        name="lru_scan",
    )(tmajor(a), tmajor(b), h0.astype(F32).reshape(rows, LANE))
    return jnp.transpose(h.reshape(S, B, C), (1, 0, 2))


NEG_BIG = -1e30


def _nt_dot(a, b):
    return lax.dot_general(a, b, (((1,), (1,)), ((), ())), preferred_element_type=F32)


def _flash_kernel(q_ref, k_ref, v_ref, *rest, scale, use_bias, tq, tk):
    if use_bias:
        cq_ref, ck_ref, o_ref, m_ref, l_ref, acc_ref = rest
    else:
        o_ref, m_ref, l_ref, acc_ref = rest
    i = pl.program_id(2)
    q = q_ref[...].astype(BF16)
    m_ref[...] = jnp.full_like(m_ref, NEG_BIG)
    l_ref[...] = jnp.zeros_like(l_ref)
    acc_ref[...] = jnp.zeros_like(acc_ref)
    row = i * tq + lax.broadcasted_iota(jnp.int32, (tq, tk), 0)
    col0 = lax.broadcasted_iota(jnp.int32, (tq, tk), 1)

    def body(j, carry):
        start = pl.multiple_of(j * tk, tk)
        ks = k_ref[pl.ds(start, tk), :].astype(BF16)
        vs = v_ref[pl.ds(start, tk), :].astype(BF16)
        s = _nt_dot(q, ks) * scale
        if use_bias:
            s = s + (cq_ref[...] - ck_ref[j])
        s = jnp.where(col0 + j * tk <= row, s, NEG_BIG)
        m_prev = m_ref[:, :1]
        m_new = jnp.maximum(m_prev, jnp.max(s, axis=-1, keepdims=True))
        alpha = jnp.exp(m_prev - m_new)
        p = jnp.exp(s - m_new)
        l_ref[...] = jnp.broadcast_to(alpha * l_ref[:, :1] + jnp.sum(p, axis=-1, keepdims=True), l_ref.shape)
        acc_ref[...] = alpha * acc_ref[...] + jnp.dot(p.astype(BF16), vs, preferred_element_type=F32)
        m_ref[...] = jnp.broadcast_to(m_new, m_ref.shape)
        return carry

    lax.fori_loop(0, i + 1, body, 0)
    o_ref[...] = acc_ref[...] / l_ref[:, :1]


def _flash_attention(q, k, v, nmaps, d, dv, maps_per_v, cum=None):
    B, S, _ = q.shape
    tq = tk = _pick(S, (256, 128, 64, 32, 16, 8))
    nq = S // tq
    use_bias = cum is not None
    in_specs = [pl.BlockSpec((None, tq, d), lambda b, m, i: (b, i, m)),
                pl.BlockSpec((None, S, d), lambda b, m, i: (b, 0, m)),
                pl.BlockSpec((None, S, dv), lambda b, m, i: (b, 0, m // maps_per_v))]
    args = [q, k, v]
    if use_bias:
        cum_t = jnp.transpose(cum, (0, 2, 1))
        in_specs += [pl.BlockSpec((None, None, tq, 1), lambda b, m, i: (b, m, i, 0)),
                     pl.BlockSpec((None, None, nq, 1, tk), lambda b, m, i: (b, m, 0, 0, 0))]
        args += [cum_t[..., None], cum_t.reshape(B, nmaps, nq, 1, tk)]
    return pl.pallas_call(
        functools.partial(_flash_kernel, scale=d ** -0.5, use_bias=use_bias, tq=tq, tk=tk),
        out_shape=jax.ShapeDtypeStruct((B, S, nmaps * dv), F32),
        grid=(B, nmaps, nq),
        in_specs=in_specs,
        out_specs=pl.BlockSpec((None, tq, dv), lambda b, m, i: (b, i, m)),
        scratch_shapes=[pltpu.VMEM((tq, LANE), F32), pltpu.VMEM((tq, LANE), F32), pltpu.VMEM((tq, dv), F32)],
        compiler_params=_cparams(("parallel", "parallel", "arbitrary")),
        name="flash_attn_bias" if use_bias else "flash_attn",
    )(*args)


def _decode_attn_kernel(pt_ref, qbd_ref, kn_ref, vn_ref, *rest, npp, use_bias):
    if use_bias:
        bias_ref, rest = rest[0], rest[1:]
    k_refs, v_refs = rest[:npp], rest[npp:2 * npp]
    o_ref, m_ref, l_ref, acc_ref = rest[2 * npp:]
    p_id = pl.program_id(1)
    qbd = qbd_ref[...]

    @pl.when(p_id == 0)
    def _():
        s_new = jnp.sum(qbd * kn_ref[...], axis=-1, keepdims=True)
        m_ref[...] = jnp.broadcast_to(s_new, m_ref.shape)
        l_ref[...] = jnp.ones_like(l_ref)
        acc_ref[...] = jnp.broadcast_to(vn_ref[...], acc_ref.shape)

    qb = qbd.astype(BF16)
    for u in range(npp):
        s = _nt_dot(qb, k_refs[u][...].astype(BF16))
        if use_bias:
            s = s + bias_ref[u]
        m_prev = m_ref[:, :1]
        m_new = jnp.maximum(m_prev, jnp.max(s, axis=-1, keepdims=True))
        alpha = jnp.exp(m_prev - m_new)
        p = jnp.exp(s - m_new)
        l_ref[...] = jnp.broadcast_to(alpha * l_ref[:, :1] + jnp.sum(p, axis=-1, keepdims=True), l_ref.shape)
        acc_ref[...] = alpha * acc_ref[...] + jnp.dot(p.astype(BF16), v_refs[u][...].astype(BF16),
                                                      preferred_element_type=F32)
        m_ref[...] = jnp.broadcast_to(m_new, m_ref.shape)

    @pl.when(p_id == pl.num_programs(1) - 1)
    def _():
        o_ref[...] = acc_ref[...] / l_ref[:, :1]


def _decode_attention(qbd, k_new, v_new, k_pool, v_pool, layer, page_table, bias=None):
    B, maps, G = qbd.shape
    n_pages = page_table.shape[1]
    page = k_pool.shape[2]
    npp = _pick(n_pages, (4, 2, 1))
    use_bias = bias is not None

    def pool_spec(u):
        return pl.BlockSpec((None, None, page, G),
                            lambda b, p, pt: (layer, pt[b * n_pages + p * npp + u], 0, 0))

    in_specs = [pl.BlockSpec((None, maps, G), lambda b, p, pt: (b, 0, 0)),
                pl.BlockSpec((None, 1, G), lambda b, p, pt: (b, 0, 0)),
                pl.BlockSpec((None, 1, G), lambda b, p, pt: (b, 0, 0))]
    args = [qbd, k_new, v_new]
    if use_bias:
        in_specs.append(pl.BlockSpec((None, npp, maps, page), lambda b, p, pt: (b, p, 0, 0)))
        args.append(bias)
    in_specs += [pool_spec(u) for u in range(npp)] + [pool_spec(u) for u in range(npp)]
    args += [k_pool] * npp + [v_pool] * npp
    grid_spec = pltpu.PrefetchScalarGridSpec(
        num_scalar_prefetch=1,
        grid=(B, n_pages // npp),
        in_specs=in_specs,
        out_specs=pl.BlockSpec((None, maps, G), lambda b, p, pt: (b, 0, 0)),
        scratch_shapes=[pltpu.VMEM((maps, LANE), F32), pltpu.VMEM((maps, LANE), F32), pltpu.VMEM((maps, G), F32)],
    )
    return pl.pallas_call(
        functools.partial(_decode_attn_kernel, npp=npp, use_bias=use_bias),
        out_shape=jax.ShapeDtypeStruct((B, maps, G), F32),
        grid_spec=grid_spec,
        compiler_params=_cparams(("parallel", "arbitrary")),
        name="decode_attn_bias" if use_bias else "decode_attn",
    )(page_table.reshape(-1).astype(jnp.int32), *args)


def _decode_rows_kernel(pt_ref, q_ref, kn_ref, vn_ref, bias_ref, *rest, npp):
    k_refs, v_refs = rest[:npp], rest[npp:2 * npp]
    o_ref, m_ref, l_ref, acc_ref = rest[2 * npp:]
    p_id = pl.program_id(1)
    q = q_ref[...]

    @pl.when(p_id == 0)
    def _():
        s_new = jnp.sum(q * kn_ref[...], axis=-1, keepdims=True)
        m_ref[...] = jnp.broadcast_to(s_new, m_ref.shape)
        l_ref[...] = jnp.ones_like(l_ref)
        acc_ref[...] = vn_ref[...]

    qb = q.astype(BF16)
    for u in range(npp):
        s = _nt_dot(qb, k_refs[u][...].astype(BF16)) + bias_ref[u]
        m_prev = m_ref[:, :1]
        m_new = jnp.maximum(m_prev, jnp.max(s, axis=-1, keepdims=True))
        alpha = jnp.exp(m_prev - m_new)
        p = jnp.exp(s - m_new)
        l_ref[...] = jnp.broadcast_to(alpha * l_ref[:, :1] + jnp.sum(p, axis=-1, keepdims=True), l_ref.shape)
        acc_ref[...] = alpha * acc_ref[...] + jnp.dot(p.astype(BF16), v_refs[u][...].astype(BF16),
                                                      preferred_element_type=F32)
        m_ref[...] = jnp.broadcast_to(m_new, m_ref.shape)

    @pl.when(p_id == pl.num_programs(1) - 1)
    def _():
        o_ref[...] = acc_ref[...] / l_ref[:, :1]


def _decode_attention_rows(q, k_new, v_new, k_pool, v_pool, layer, page_table, bias):
    B, heads, d = q.shape
    dv = v_new.shape[2]
    n_pages = page_table.shape[1]
    rows = k_pool.shape[2]
    npp = _pick(n_pages, (4, 2, 1))

    def pool_spec(u, w):
        return pl.BlockSpec((None, None, rows, w),
                            lambda b, p, pt: (layer, pt[b * n_pages + p * npp + u], 0, 0))

    in_specs = [pl.BlockSpec((None, heads, d), lambda b, p, pt: (b, 0, 0)),
                pl.BlockSpec((None, heads, d), lambda b, p, pt: (b, 0, 0)),
                pl.BlockSpec((None, heads, dv), lambda b, p, pt: (b, 0, 0)),
                pl.BlockSpec((None, npp, heads, rows), lambda b, p, pt: (b, p, 0, 0))]
    in_specs += [pool_spec(u, d) for u in range(npp)] + [pool_spec(u, dv) for u in range(npp)]
    grid_spec = pltpu.PrefetchScalarGridSpec(
        num_scalar_prefetch=1,
        grid=(B, n_pages // npp),
        in_specs=in_specs,
        out_specs=pl.BlockSpec((None, heads, dv), lambda b, p, pt: (b, 0, 0)),
        scratch_shapes=[pltpu.VMEM((heads, LANE), F32), pltpu.VMEM((heads, LANE), F32),
                        pltpu.VMEM((heads, dv), F32)],
    )
    return pl.pallas_call(
        functools.partial(_decode_rows_kernel, npp=npp),
        out_shape=jax.ShapeDtypeStruct((B, heads, dv), F32),
        grid_spec=grid_spec,
        compiler_params=_cparams(("parallel", "arbitrary")),
        name="decode_attn_rows",
    )(page_table.reshape(-1).astype(jnp.int32), q, k_new, v_new, bias, *([k_pool] * npp), *([v_pool] * npp))


def _block_diag_queries(q, scale):
    B, maps, d = q.shape
    eye = jnp.eye(maps, dtype=q.dtype)
    return (eye[None, :, :, None] * (q * scale)[:, None, :, :]).reshape(B, maps, maps * d)


def _diag_blocks(o, nblocks):
    B, maps, G = o.shape
    w = G // nblocks
    per = maps // nblocks
    o = o.reshape(B, nblocks, per, nblocks, w)
    idx = jnp.arange(nblocks)
    return o[:, idx, :, idx].transpose(1, 0, 2, 3).reshape(B, maps, w)


def _rmsnorm(x, g, eps=NORM_EPS):
    xf = x.astype(F32)
    y = xf * lax.rsqrt(jnp.mean(xf * xf, axis=-1, keepdims=True) + eps)
    return y * g.astype(F32)


def _apply_partial_rope(x, pos, rope_dims):
    half = rope_dims // 2
    inv = ROPE_THETA ** (-jnp.arange(half, dtype=F32) * (2.0 / rope_dims))
    ang = pos.astype(F32)[:, None] * inv[None, :]
    cos = jnp.cos(ang)[None, :, None, None, :]
    sin = jnp.sin(ang)[None, :, None, None, :]
    x1, x2 = x[..., :half], x[..., half:rope_dims]
    return jnp.concatenate([x1 * cos - x2 * sin, x2 * cos + x1 * sin, x[..., rope_dims:]], axis=-1)


def _rwkv7_time_mix(zr, shift0, wkv0, mu, w0, w2, a0, a2, g2, k_k, k_a, r_k, ln_w, ln_b):
    B, S, _ = zr.shape
    G = w0.shape[0]
    N = RWKV_HS
    H = G // N
    dl, al = w2.shape[0], a2.shape[0]
    prev = jnp.concatenate([shift0[:, None], zr[:, :-1]], axis=1)
    zs = zr + (prev - zr) * mu
    o1 = 3 * G
    o2 = o1 + dl
    o3 = o2 + al
    r, k, v = zs[..., :G], zs[..., G:2 * G], zs[..., 2 * G:3 * G]
    w_raw = -jax.nn.softplus(-(w0 + jnp.tanh(zs[..., o1:o2]) @ w2)) - 0.5
    decay = jnp.exp(-jnp.exp(w_raw))
    a = jax.nn.sigmoid(a0 + zs[..., o2:o3] @ a2)
    g = jax.nn.sigmoid(zs[..., o3:]) @ g2
    kk = (k * k_k).reshape(B, S, H, N)
    kk = kk / jnp.maximum(jnp.linalg.norm(kk, axis=-1, keepdims=True), 1e-12)
    k = (k * (1.0 + (a - 1.0) * k_a)).reshape(B, S, H, N)
    r = r.reshape(B, S, H, N)
    v = v.reshape(B, S, H, N)
    decay = decay.reshape(B, S, H, N)
    a = a.reshape(B, S, H, N)

    out, wkv_n = _rwkv_scan(r, decay, k, v, kk, a, wkv0)
    mean = jnp.mean(out, axis=-1, keepdims=True)
    var = jnp.mean(jnp.square(out - mean), axis=-1, keepdims=True)
    out = (out - mean) * lax.rsqrt(var + RWKV_GN_EPS) * ln_w.reshape(H, N) + ln_b.reshape(H, N)
    out = out + jnp.sum(r * k * r_k, axis=-1, keepdims=True) * v
    y = out.reshape(B, S, G) * g
    return y, zr[:, -1], wkv_n


def _rg_lru_block(zx, zgate, conv0, h0, conv_w, conv_b, wa, ba, wx, bx, lam):
    B, S, C = zx.shape
    cw = conv_w.shape[0]
    nblocks, bs = wa.shape[0], wa.shape[1]
    xpad = jnp.concatenate([conv0, zx], axis=1)
    xc = lax.conv_general_dilated(xpad, conv_w[:, None, :], window_strides=(1,),
                                  padding='VALID', dimension_numbers=('NWC', 'WIO', 'NWC'),
                                  feature_group_count=C) + conv_b
    xb = xc.reshape(B, S, nblocks, bs)
    r = jax.nn.sigmoid(jnp.einsum('bsnc,ncd->bsnd', xb, wa).reshape(B, S, C) + ba)
    i = jax.nn.sigmoid(jnp.einsum('bsnc,ncd->bsnd', xb, wx).reshape(B, S, C) + bx)
    log_a = -LRU_C * r * jax.nn.softplus(-lam)
    a = jnp.exp(log_a)
    b = jnp.sqrt(-jnp.expm1(2.0 * log_a)) * (i * xc)
    h = _lru_scan(a, b, h0)
    y = h * jax.nn.gelu(zgate)
    return y, xpad[:, -(cw - 1):], h[:, -1]


def _differential_attention(zq, zk, zv, pos, lq1, lk1, lq2, lk2, subln_g, layer, heads, paged=None):
    B, S, G = zq.shape
    dh = G // (2 * heads)
    vd = 2 * dh
    rope_dims = dh // 4
    q = _apply_partial_rope(zq.reshape(B, S, heads, 2, dh), pos, rope_dims)
    k = _apply_partial_rope(zk.reshape(B, S, heads, 2, dh), pos, rope_dims)
    v = zv.reshape(B, S, heads, vd)
    if paged is None:
        o = _flash_attention(q.reshape(B, S, G), k.reshape(B, S, G), zv, 2 * heads, dh, vd, 2)
        o = o.reshape(B, S, heads, 2, vd)
    else:
        assert S == 1
        k_pool, v_pool, page_table = paged
        qbd = _block_diag_queries(q.reshape(B, 2 * heads, dh), dh ** -0.5)
        o = _decode_attention(qbd, k.reshape(B, 1, G), zv, k_pool.reshape(k_pool.shape[:3] + (G,)),
                              v_pool.reshape(v_pool.shape[:3] + (G,)), layer, page_table)
        o = _diag_blocks(o, heads).reshape(B, 1, heads, 2, vd)
    lam_init = 0.8 - 0.6 * math.exp(-0.3 * layer)
    lam = jnp.exp(jnp.sum(lq1 * lk1)) - jnp.exp(jnp.sum(lq2 * lk2)) + lam_init
    o = o[..., 0, :] - lam * o[..., 1, :]
    o = _rmsnorm(o, subln_g) * (1.0 - lam_init)
    return o.reshape(B, S, G), k.reshape(B, S, heads, 2 * dh), v


def _forgetting_attention(zq, zk, zv, zf, b_f, layer, heads, paged=None):
    B, S, G = zq.shape
    dh = G // heads
    k = zk.reshape(B, S, heads, dh)
    v = zv.reshape(B, S, heads, dh)
    logf = jax.nn.log_sigmoid(zf + b_f)
    if paged is None:
        cum = jnp.cumsum(logf, axis=1)
        o = _flash_attention(zq, zk, zv, heads, dh, dh, 1, cum=cum)
    else:
        assert S == 1
        k_pool, v_pool, logf_pool, page_table = paged
        n_pages, page = page_table.shape[1], k_pool.shape[2]
        lf_all = jnp.concatenate([_gather_pages(logf_pool, layer, page_table), logf], axis=1)
        cum = jnp.cumsum(lf_all, axis=1)
        bias = (cum[:, -1:] - cum[:, :-1]).reshape(B, n_pages, 1, page, heads)
        own = jnp.eye(heads, dtype=bool)[None, None, :, None, :]
        bias = jnp.where(own, bias, NEG_BIG).reshape(B, n_pages, heads, page * heads)
        pool_shape = k_pool.shape[:2] + (page * heads, dh)
        o = _decode_attention_rows(zq.reshape(B, heads, dh) * dh ** -0.5, zk.reshape(B, heads, dh),
                                   zv.reshape(B, heads, dh), k_pool.reshape(pool_shape),
                                   v_pool.reshape(pool_shape), layer, page_table, bias)
        o = o.reshape(B, 1, G)
    return o, k, v, logf


def _gather_pages(pool, layer, page_table):
    g = pool[layer, page_table]
    return g.reshape((g.shape[0], g.shape[1] * g.shape[2]) + g.shape[3:])


def _moe_route(xn, n, w_router, b_router, n_experts, blk):
    logits = jnp.dot(xn[:n], w_router, precision=lax.Precision.HIGHEST) + b_router
    top_val, top_idx = lax.top_k(logits, TOP_K)
    gates = jax.nn.softmax(top_val, axis=-1)
    m = n * TOP_K
    nblk = -(-m // blk) + n_experts
    e_flat = top_idx.reshape(-1)
    tok = jnp.arange(m, dtype=jnp.int32) // TOP_K
    order = jnp.argsort(e_flat)
    e_sorted = e_flat[order]
    counts = jnp.bincount(e_flat, length=n_experts)
    pcounts = ((counts + blk - 1) // blk) * blk
    pend = jnp.cumsum(pcounts)
    pstart = pend - pcounts
    start = jnp.cumsum(counts) - counts
    dest = (pstart[e_sorted] + (jnp.arange(m) - start[e_sorted])).astype(jnp.int32)
    slot_tok = jnp.full((nblk * blk,), n, jnp.int32).at[dest].set(tok[order])
    slot_gate = jnp.zeros((nblk * blk,), F32).at[dest].set(gates.reshape(-1)[order])
    block_e = jnp.minimum(jnp.searchsorted(pend, jnp.arange(nblk) * blk, side='right'),
                          n_experts - 1).astype(jnp.int32)
    pos = jnp.zeros((m,), jnp.int32).at[order].set(dest).reshape(n, TOP_K)
    n_used = (pend[-1] // blk).astype(jnp.int32).reshape(1)
    return slot_tok, slot_gate, block_e, n_used, pos


def kernel(x_prompt, x_sample, cache_diff_k, cache_diff_v, cache_fox_k, cache_fox_v, cache_fox_logf, state_rwkv_shift, state_rwkv_wkv, state_lru_conv, state_lru_h, page_table, norm_mix, norm_ffn, norm_final, w_in, w_out, rwkv_mu, rwkv_w0, rwkv_w2, rwkv_a0, rwkv_a2, rwkv_g2, rwkv_k_k, rwkv_k_a, rwkv_r_k, rwkv_ln_w, rwkv_ln_b, lru_conv_w, lru_conv_b, lru_wa, lru_ba, lru_wx, lru_bx, lru_lambda, diff_lq1, diff_lk1, diff_lq2, diff_lk2, diff_subln, fox_bf, ffn_w_gate, ffn_w_up, ffn_w_down, moe_router, moe_router_b, moe_w_gate, moe_w_up, moe_w_down):
    bp, sp, D = x_prompt.shape
    bs, ss, _ = x_sample.shape
    depth = w_in.shape[0]
    n_in = w_in.shape[2]
    G = D // 4
    rproj = rwkv_mu.shape[1]
    diff_heads = cache_diff_k.shape[3]
    fox_heads = cache_fox_k.shape[3]
    n_experts = moe_w_gate.shape[1]
    cw = lru_conv_w.shape[1]
    past_len = page_table.shape[1] * cache_diff_k.shape[2]

    n_p = bp * sp
    n_s = bs * ss
    n_tok = n_p + n_s
    m_all = -(-n_tok // LANE) * LANE
    n_in_pad = -(-n_in // 512) * 512

    x_all = jnp.concatenate([x_prompt.reshape(n_p, D), x_sample.reshape(n_s, D),
                             jnp.zeros((m_all - n_tok, D), F32)], axis=0)

    pos_p = jnp.arange(sp, dtype=jnp.int32)
    pos_s = past_len + jnp.arange(ss, dtype=jnp.int32)

    offs = {}
    off = 0
    for name, width in (('rwkv', rproj), ('lru_x', G), ('lru_gate', G), ('diff_q', G), ('diff_k', G),
                        ('diff_v', G), ('fox_q', G), ('fox_k', G), ('fox_v', G), ('fox_f', fox_heads)):
        offs[name] = (off, off + width)
        off += width

    def col(z, name):
        a, b = offs[name]
        return z[..., a:b]

    def mixers(z, l, pos, shift0, wkv0, conv0, lru_h0, sample):
        ya, shift_n, wkv_n = _rwkv7_time_mix(
            col(z, 'rwkv'), shift0, wkv0, rwkv_mu[l], rwkv_w0[l], rwkv_w2[l], rwkv_a0[l], rwkv_a2[l],
            rwkv_g2[l], rwkv_k_k[l], rwkv_k_a[l], rwkv_r_k[l], rwkv_ln_w[l], rwkv_ln_b[l])
        yb, conv_n, lru_hn = _rg_lru_block(
            col(z, 'lru_x'), col(z, 'lru_gate'), conv0, lru_h0, lru_conv_w[l], lru_conv_b[l],
            lru_wa[l], lru_ba[l], lru_wx[l], lru_bx[l], lru_lambda[l])
        yc, dk, dv = _differential_attention(
            col(z, 'diff_q'), col(z, 'diff_k'), col(z, 'diff_v'), pos,
            diff_lq1[l], diff_lk1[l], diff_lq2[l], diff_lk2[l], diff_subln[l], l, diff_heads,
            paged=(cache_diff_k, cache_diff_v, page_table) if sample else None)
        yd, fk, fv, fl = _forgetting_attention(
            col(z, 'fox_q'), col(z, 'fox_k'), col(z, 'fox_v'), col(z, 'fox_f'), fox_bf[l], l, fox_heads,
            paged=(cache_fox_k, cache_fox_v, cache_fox_logf, page_table) if sample else None)
        y = jnp.concatenate([ya, yb, yc, yd], axis=-1)
        return y, (dk, dv, fk, fv, fl, shift_n, wkv_n, conv_n, lru_hn)

    new_p, new_s = [], []
    for l in range(depth):
        w_in_l = jnp.pad(w_in[l], ((0, 0), (0, n_in_pad - n_in))).astype(BF16)
        hn = _rmsnorm(x_all, norm_mix[l]).astype(BF16)
        z_all = _mm(hn, w_in_l)
        z_p = z_all[:n_p, :n_in].reshape(bp, sp, n_in)
        z_s = z_all[n_p:n_tok, :n_in].reshape(bs, ss, n_in)

        prompt_state = (jnp.zeros((bp, rproj), F32), jnp.zeros((bp, G // RWKV_HS, RWKV_HS, RWKV_HS), F32),
                        jnp.zeros((bp, cw - 1, G), F32), jnp.zeros((bp, G), F32))
        y_p, st_p = mixers(z_p, l, pos_p, *prompt_state, False)
        y_s, st_s = mixers(z_s, l, pos_s, state_rwkv_shift[l], state_rwkv_wkv[l], state_lru_conv[l],
                           state_lru_h[l], True)
        new_p.append(st_p)
        new_s.append(st_s)

        y_all = jnp.concatenate([y_p.reshape(n_p, D), y_s.reshape(n_s, D),
                                 jnp.zeros((m_all - n_tok, D), F32)], axis=0).astype(BF16)
        x_all = _mm(y_all, w_out[l].astype(BF16), resid=x_all)

        hf32 = _rmsnorm(x_all, norm_ffn[l])
        hf = hf32.astype(BF16)
        i = l // 2
        if l % 2 == 0:
            hmid = _gateup(hf, ffn_w_gate[i].astype(BF16), ffn_w_up[i].astype(BF16))
            x_all = _down(hmid, ffn_w_down[i].astype(BF16), x_all)
        else:
            blk = 512 if n_tok * TOP_K >= 4096 else 128
            slot_tok, slot_gate, block_e, n_used, pos = _moe_route(
                hf32, n_tok, moe_router[i], moe_router_b[i], n_experts, blk)
            hf_pad = jnp.concatenate([hf[:n_tok], jnp.zeros((1, D), BF16)], axis=0)
            xs = hf_pad[slot_tok]
            hmid = _moe_gateup(block_e, n_used, xs, moe_w_gate[i].astype(BF16), moe_w_up[i].astype(BF16), blk)
            ys = _moe_down(block_e, n_used, hmid, moe_w_down[i].astype(BF16), slot_gate[:, None], blk)
            y_tok = ys[pos[:, 0]] + ys[pos[:, 1]]
            x_all = x_all + jnp.concatenate([y_tok, jnp.zeros((m_all - n_tok, D), F32)], axis=0)

    y_all = _rmsnorm(x_all, norm_final)
    y_prompt = y_all[:n_p].reshape(bp, sp, D)
    y_sample = y_all[n_p:n_tok].reshape(bs, ss, D)
    outs_p = [jnp.stack([s[i] for s in new_p]) for i in range(9)]
    outs_s = [jnp.stack([s[i] for s in new_s]) for i in range(9)]
    return (y_prompt, y_sample, *outs_p, *outs_s)
```
